```python
import math
import jax, jax.numpy as jnp
from jax import lax
import numpy as np

D_MODEL = 1024
BATCH = 8
SEQ = 4096
DEPTH = 1

CHUNK = 64
Q_BLOCK = 128
SSM_WIDTH = 1024
SSM_GROUP = 16
SSM_GROUPS = SSM_WIDTH // SSM_GROUP
SSM_STATE = 64
DT_MIN = 1e-3
DT_MAX = 1e-1
MLA_HEADS = 16
QK_NOPE = 64
QK_ROPE = 32
V_HEAD = 64
Q_LORA = 256
KV_LORA = 256
MLA_WIDTH = MLA_HEADS * V_HEAD
ROPE_BASE = 10000.0
EPS = 1e-6
IN_SPLITS = (SSM_WIDTH, SSM_WIDTH, Q_LORA, KV_LORA, QK_ROPE, MLA_WIDTH, D_MODEL, D_MODEL)
IN_WIDTH = sum(IN_SPLITS)

kernel_name = "hybrid_s5_mla_gated_block"


def _rmsnorm(x, g):
    xf = x.astype(jnp.float32)
    y = xf * lax.rsqrt(jnp.mean(xf * xf, axis=-1, keepdims=True) + EPS)
    return (y * g.astype(jnp.float32)).astype(x.dtype)


def _cmul(ar, ai, br, bi):
    return ar * br - ai * bi, ar * bi + ai * br


def _ssm_branch(u, log_dt, a_re, a_im, b_re, b_im, c_re, c_im, d_skip):
    f32 = jnp.float32
    out_dtype = u.dtype
    bsz, seq, _ = u.shape
    n_chunks = seq // CHUNK
    u = u.astype(f32).reshape(bsz, n_chunks, CHUNK, SSM_GROUPS, SSM_GROUP)
    u = jnp.moveaxis(u, 1, 0)
    dt = jnp.exp(log_dt.astype(f32))[:, None]
    lr, li = a_re.astype(f32), a_im.astype(f32)
    mag = jnp.exp(lr * dt)
    abar_re, abar_im = mag * jnp.cos(li * dt), mag * jnp.sin(li * dt)
    den = lr * lr + li * li
    nr, ni = abar_re - 1.0, abar_im
    fr = (nr * lr + ni * li) / den
    fi = (ni * lr - nr * li) / den
    br, bi = b_re.astype(f32), b_im.astype(f32)
    bbar_re, bbar_im = _cmul(fr[..., None], fi[..., None], br, bi)
    cr_w, ci_w = c_re.astype(f32), c_im.astype(f32)
    d = d_skip.astype(f32).reshape(SSM_GROUPS, SSM_GROUP)

    def combine(e1, e2):
        a1r, a1i, b1r, b1i = e1
        a2r, a2i, b2r, b2i = e2
        ar, ai = _cmul(a2r, a2i, a1r, a1i)
        xr, xi = _cmul(a2r, a2i, b1r, b1i)
        return ar, ai, xr + b2r, xi + b2i

    def chunk_step(carry, u_c):
        sr, si = carry
        bur = jnp.einsum('btgp,gnp->btgn', u_c, bbar_re)
        bui = jnp.einsum('btgp,gnp->btgn', u_c, bbar_im)
        ar = jnp.broadcast_to(abar_re, bur.shape)
        ai = jnp.broadcast_to(abar_im, bur.shape)
        pr, pi_, xr, xi = lax.associative_scan(combine, (ar, ai, bur, bui), axis=1)
        cr, ci = _cmul(pr, pi_, sr[:, None], si[:, None])
        xr, xi = xr + cr, xi + ci
        y = (jnp.einsum('btgn,gpn->btgp', xr, cr_w)
             - jnp.einsum('btgn,gpn->btgp', xi, ci_w) + d * u_c)
        return (xr[:, -1], xi[:, -1]), y

    init = (jnp.zeros((bsz, SSM_GROUPS, SSM_STATE), f32),
            jnp.zeros((bsz, SSM_GROUPS, SSM_STATE), f32))
    _, ys = lax.scan(chunk_step, init, u)
    return jnp.moveaxis(ys, 0, 1).reshape(bsz, seq, SSM_WIDTH).astype(out_dtype)


def _rope(t, cos, sin):
    half = t.shape[-1] // 2
    t1, t2 = t[..., :half], t[..., half:]
    return jnp.concatenate([t1 * cos - t2 * sin, t2 * cos + t1 * sin], axis=-1).astype(t.dtype)


def _mla_branch(q_lat, kv_lat, k_rope, positions, g_q_norm, w_q_up, g_kv_norm, w_kv_up):
    bsz, seq, _ = q_lat.shape
    q = (_rmsnorm(q_lat, g_q_norm) @ w_q_up).reshape(bsz, seq, MLA_HEADS, QK_NOPE + QK_ROPE)
    q_nope, q_rope = q[..., :QK_NOPE], q[..., QK_NOPE:]
    kv = (_rmsnorm(kv_lat, g_kv_norm) @ w_kv_up).reshape(bsz, seq, MLA_HEADS, QK_NOPE + V_HEAD)
    k_nope, v = kv[..., :QK_NOPE], kv[..., QK_NOPE:]
    inv_freq = ROPE_BASE ** (-jnp.arange(0, QK_ROPE, 2, dtype=jnp.float32) / QK_ROPE)
    ang = positions.astype(jnp.float32)[..., None] * inv_freq
    cos, sin = jnp.cos(ang), jnp.sin(ang)
    q_rope = _rope(q_rope, cos[:, :, None], sin[:, :, None])
    k_rope = _rope(k_rope, cos, sin)
    n_blk = seq // Q_BLOCK
    key_chunk = jnp.arange(seq) // CHUNK
    scale = (QK_NOPE + QK_ROPE) ** -0.5

    def to_blocks(t):
        return jnp.moveaxis(t.reshape(bsz, n_blk, Q_BLOCK, *t.shape[2:]), 1, 0)

    def attend(args):
        qn, qr, blk = args
        s = (jnp.einsum('bqhd,bkhd->bhqk', qn, k_nope)
             + jnp.einsum('bqhr,bkr->bhqk', qr, k_rope)).astype(jnp.float32) * scale
        q_chunk = (blk * Q_BLOCK + jnp.arange(Q_BLOCK)) // CHUNK
        mask = key_chunk[None, :] <= q_chunk[:, None]
        p = jax.nn.softmax(jnp.where(mask, s, -jnp.inf), axis=-1).astype(v.dtype)
        return jnp.einsum('bhqk,bkhd->bqhd', p, v)

    o = lax.map(attend, (to_blocks(q_nope), to_blocks(q_rope), jnp.arange(n_blk)))
    return jnp.moveaxis(o, 0, 1).reshape(bsz, seq, MLA_WIDTH)


def setup_inputs(seed: int = 0) -> dict:
    key = jax.random.key(seed)
    ks = jax.random.split(key, 25)
    f32 = jnp.float32

    def nrm(k, shape, scale):
        return jax.random.normal(k, shape, f32) * scale

    positions = (jax.random.randint(ks[2], (BATCH, 1), 0, 2048, dtype=jnp.int32)
                 + jnp.arange(SEQ, dtype=jnp.int32)[None, :])
    return {
        "x": nrm(ks[0], (BATCH, SEQ, D_MODEL), 1.0),
        "c": nrm(ks[1], (BATCH, D_MODEL), 1.0),
        "positions": positions,
        "w_ada": nrm(ks[3], (DEPTH, D_MODEL, 3 * D_MODEL), 0.5 * D_MODEL ** -0.5),
        "b_ada": nrm(ks[4], (DEPTH, 3 * D_MODEL), 0.01),
        "g_pre": 1.0 + nrm(ks[5], (DEPTH, D_MODEL), 0.01),
        "w_in": nrm(ks[6], (DEPTH, D_MODEL, IN_WIDTH), D_MODEL ** -0.5),
        "ssm_log_dt": jax.random.uniform(ks[7], (DEPTH, SSM_GROUPS), f32,
                                         math.log(DT_MIN), math.log(DT_MAX)),
        "ssm_a_re": -0.5 + nrm(ks[8], (DEPTH, SSM_GROUPS, SSM_STATE), 0.01),
        "ssm_a_im": (jnp.pi * jnp.arange(SSM_STATE, dtype=f32)
                     + nrm(ks[9], (DEPTH, SSM_GROUPS, SSM_STATE), 0.01)),
        "ssm_b_re": nrm(ks[10], (DEPTH, SSM_GROUPS, SSM_STATE, SSM_GROUP), (2 * SSM_GROUP) ** -0.5),
        "ssm_b_im": nrm(ks[11], (DEPTH, SSM_GROUPS, SSM_STATE, SSM_GROUP), (2 * SSM_GROUP) ** -0.5),
        "ssm_c_re": nrm(ks[12], (DEPTH, SSM_GROUPS, SSM_GROUP, SSM_STATE), (2 * SSM_STATE) ** -0.5),
        "ssm_c_im": nrm(ks[13], (DEPTH, SSM_GROUPS, SSM_GROUP, SSM_STATE), (2 * SSM_STATE) ** -0.5),
        "ssm_d": nrm(ks[14], (DEPTH, SSM_WIDTH), 0.5),
        "w_glu": nrm(ks[15], (DEPTH, SSM_WIDTH, 2 * SSM_WIDTH), SSM_WIDTH ** -0.5),
        "b_glu": nrm(ks[16], (DEPTH, 2 * SSM_WIDTH), 0.01),
        "g_q_norm": 1.0 + nrm(ks[17], (DEPTH, Q_LORA), 0.01),
        "w_q_up": nrm(ks[18], (DEPTH, Q_LORA, MLA_HEADS * (QK_NOPE + QK_ROPE)), Q_LORA ** -0.5),
        "g_kv_norm": 1.0 + nrm(ks[19], (DEPTH, KV_LORA), 0.01),
        "w_kv_up": nrm(ks[20], (DEPTH, KV_LORA, MLA_HEADS * (QK_NOPE + V_HEAD)), KV_LORA ** -0.5),
        "w_br_ssm": nrm(ks[21], (DEPTH, SSM_WIDTH, D_MODEL), SSM_WIDTH ** -0.5),
        "w_br_mla": nrm(ks[22], (DEPTH, MLA_WIDTH, D_MODEL), MLA_WIDTH ** -0.5),
        "w_out": nrm(ks[23], (DEPTH, D_MODEL, D_MODEL), D_MODEL ** -0.5),
        "g_post": 1.0 + nrm(ks[24], (DEPTH, D_MODEL), 0.01),
    }


def reference(x, c, positions, w_ada, b_ada, g_pre, w_in, ssm_log_dt, ssm_a_re, ssm_a_im,
              ssm_b_re, ssm_b_im, ssm_c_re, ssm_c_im, ssm_d, w_glu, b_glu, g_q_norm, w_q_up,
              g_kv_norm, w_kv_up, w_br_ssm, w_br_mla, w_out, g_post):
    split_at = np.cumsum(IN_SPLITS)[:-1].tolist()
    for l in range(DEPTH):
        mod = c @ w_ada[l] + b_ada[l]
        shift, scale, gate = jnp.split(mod, 3, axis=-1)
        h = _rmsnorm(x, g_pre[l]) * (1.0 + scale[:, None]) + shift[:, None]
        proj = h @ w_in[l]
        u_s, z_s, q_lat, kv_lat, k_rope, z_m, gl_s, gl_m = jnp.split(proj, split_at, axis=-1)
        y_s = _ssm_branch(u_s, ssm_log_dt[l], ssm_a_re[l], ssm_a_im[l], ssm_b_re[l],
                          ssm_b_im[l], ssm_c_re[l], ssm_c_im[l], ssm_d[l])
        glu_a, glu_b = jnp.split(jax.nn.gelu(y_s) @ w_glu[l] + b_glu[l], 2, axis=-1)
        y_s = ((glu_a * jax.nn.sigmoid(glu_b)) * jax.nn.silu(z_s)) @ w_br_ssm[l]
        y_m = _mla_branch(q_lat, kv_lat, k_rope, positions, g_q_norm[l], w_q_up[l],
                          g_kv_norm[l], w_kv_up[l])
        y_m = (y_m * jax.nn.silu(z_m)) @ w_br_mla[l]
        merged = jax.nn.sigmoid(gl_s) * y_s + jax.nn.sigmoid(gl_m) * y_m
        out = merged @ w_out[l]
        x = x + gate[:, None] * _rmsnorm(out, g_post[l])
    return x
```

```python
import functools
import math

import jax
import jax.numpy as jnp
import numpy as np
from jax import lax
from jax.experimental import pallas as pl
from jax.experimental.pallas import tpu as pltpu

D_MODEL = 1024
CHUNK = 64
SSM_WIDTH = 1024
SSM_GROUP = 16
SSM_GROUPS = SSM_WIDTH // SSM_GROUP
SSM_STATE = 64
MLA_HEADS = 16
QK_NOPE = 64
QK_ROPE = 32
V_HEAD = 64
Q_LORA = 256
KV_LORA = 256
MLA_WIDTH = MLA_HEADS * V_HEAD
ROPE_BASE = 10000.0
EPS = 1e-6

LANES = 128
SUBLANES = 8
HEAD_PAD = 128
LAT_PAD = 640
SSM_SUB = 16
SSM_PAIRS = SSM_GROUPS // 2
VMEM_LIMIT = 56 * 1024 * 1024

_F32 = jnp.float32
_BF16 = jnp.bfloat16
_SOFTMAX_SCALE_LOG2E = ((QK_NOPE + QK_ROPE) ** -0.5) * math.log2(math.e)
_NEG_BIG = -1e30


def _const_spec(shape):
    zeros = (0,) * len(shape)
    return pl.BlockSpec(shape, lambda *_: zeros, pipeline_mode=pl.Buffered(1))


def _sigmoid(x):
    return 1.0 / (1.0 + jnp.exp(-x))


def _gelu_tanh(x):
    return 0.5 * x * (1.0 + jnp.tanh(math.sqrt(2.0 / math.pi) * (x + 0.044715 * (x * x * x))))


def _rms(x):
    return x * lax.rsqrt(jnp.mean(x * x, axis=-1, keepdims=True) + EPS)


def _ada_kernel(c_ref, w_ref, b_ref, o_ref):
    o_ref[...] = jnp.dot(c_ref[...], w_ref[...], preferred_element_type=_F32) + b_ref[...]


def _ada(c, w, b):
    bsz = c.shape[0]
    n = w.shape[1]
    bn = D_MODEL
    return pl.pallas_call(
        _ada_kernel,
        grid=(n // bn,),
        in_specs=[pl.BlockSpec((bsz, D_MODEL), lambda j: (0, 0)),
                  pl.BlockSpec((D_MODEL, bn), lambda j: (0, j)),
                  pl.BlockSpec((1, bn), lambda j: (0, j))],
        out_specs=pl.BlockSpec((bsz, bn), lambda j: (0, j)),
        out_shape=jax.ShapeDtypeStruct((bsz, n), _F32),
        compiler_params=pltpu.CompilerParams(dimension_semantics=("arbitrary",),
                                             vmem_limit_bytes=VMEM_LIMIT),
        name="ada",
    )(c, w, b.reshape(1, n))


def _inproj_kernel(x_ref, shift_ref, scale_ref, gpre_ref, pos_ref, invf_ref,
                   w_u_ref, w_zs_ref, w_zm_ref, w_gs_ref, w_gm_ref, w_lat_ref,
                   gq_ref, gkv_ref, wq_ref, wqr_ref, wk_ref, wkr_ref, wkrr_ref, wv_ref,
                   u_ref, szs_ref, szm_ref, sgs_ref, sgm_ref, q_ref, k_ref, v_ref):
    x = x_ref[0]
    h = _rms(x) * gpre_ref[...] * (1.0 + scale_ref[0]) + shift_ref[0]
    hb = h.astype(_BF16)

    def proj(w_ref):
        return jnp.dot(hb, w_ref[...], preferred_element_type=_F32)

    u_ref[0] = proj(w_u_ref).astype(_BF16)
    z = proj(w_zs_ref)
    szs_ref[0] = (z * _sigmoid(z)).astype(_BF16)
    z = proj(w_zm_ref)
    szm_ref[0] = (z * _sigmoid(z)).astype(_BF16)
    sgs_ref[0] = _sigmoid(proj(w_gs_ref)).astype(_BF16)
    sgm_ref[0] = _sigmoid(proj(w_gm_ref)).astype(_BF16)

    lat = proj(w_lat_ref)
    qn = (_rms(lat[:, :Q_LORA]) * gq_ref[...]).astype(_BF16)
    kvn = (_rms(lat[:, Q_LORA:Q_LORA + KV_LORA]) * gkv_ref[...]).astype(_BF16)
    kr = lat[:, Q_LORA + KV_LORA:].astype(_BF16)

    ang = pos_ref[0].astype(_F32) * invf_ref[...]
    cos = jnp.cos(ang)
    sin = jnp.sin(ang)

    qa = jnp.dot(qn, wq_ref[...], preferred_element_type=_F32)
    qb = jnp.dot(qn, wqr_ref[...], preferred_element_type=_F32)
    ka = (jnp.dot(kvn, wk_ref[...], preferred_element_type=_F32)
          + jnp.dot(kr, wkr_ref[...], preferred_element_type=_F32))
    kb = jnp.dot(kr, wkrr_ref[...], preferred_element_type=_F32)
    for hd in range(MLA_HEADS):
        sl = slice(hd * HEAD_PAD, (hd + 1) * HEAD_PAD)
        q_ref[0, hd] = ((qa[:, sl] * cos + qb[:, sl] * sin) * _SOFTMAX_SCALE_LOG2E).astype(_BF16)
        k_ref[0, hd] = (ka[:, sl] * cos + kb[:, sl] * sin).astype(_BF16)
    v_ref[0] = jnp.dot(kvn, wv_ref[...], preferred_element_type=_F32).astype(_BF16)


def _inproj(x, shift, scale, g_pre, pos3, invf, ws, tm):
    bsz, seq, _ = x.shape
    tok = lambda w: pl.BlockSpec((1, tm, w), lambda b, i: (b, i, 0))
    per_b = pl.BlockSpec((1, 1, D_MODEL), lambda b, i: (b, 0, 0))
    head = pl.BlockSpec((1, MLA_HEADS, tm, HEAD_PAD), lambda b, i: (b, 0, i, 0))
    act = jax.ShapeDtypeStruct((bsz, seq, D_MODEL), _BF16)
    hshape = jax.ShapeDtypeStruct((bsz, MLA_HEADS, seq, HEAD_PAD), _BF16)
    w_specs = [_const_spec(w.shape) for w in ws]
    return pl.pallas_call(
        _inproj_kernel,
        grid=(bsz, seq // tm),
        in_specs=[tok(D_MODEL), per_b, per_b, _const_spec((1, D_MODEL)), tok(1),
                  _const_spec((1, HEAD_PAD))] + w_specs,
        out_specs=[tok(D_MODEL)] * 5 + [head, head, tok(MLA_WIDTH)],
        out_shape=[act] * 5 + [hshape, hshape, act],
        compiler_params=pltpu.CompilerParams(dimension_semantics=("parallel", "parallel"),
                                             vmem_limit_bytes=VMEM_LIMIT),
        name="inproj",
    )(x, shift, scale, g_pre, pos3, invf, *ws)


def _ssm_weights(log_dt, a_re, a_im, b_re, b_im, c_re, c_im, d_skip):
    hi = lax.Precision.HIGHEST
    L, G, N, P = SSM_SUB, SSM_GROUPS, SSM_STATE, SSM_GROUP
    dt = jnp.exp(log_dt.astype(_F32))[:, None]
    lr, li = a_re.astype(_F32), a_im.astype(_F32)
    mag = jnp.exp(lr * dt)
    abar_re, abar_im = mag * jnp.cos(li * dt), mag * jnp.sin(li * dt)
    den = lr * lr + li * li
    nr, ni = abar_re - 1.0, abar_im
    fr = (nr * lr + ni * li) / den
    fi = (ni * lr - nr * li) / den
    br, bi = b_re.astype(_F32), b_im.astype(_F32)
    bbr = fr[..., None] * br - fi[..., None] * bi
    bbi = fr[..., None] * bi + fi[..., None] * br
    cr, ci = c_re.astype(_F32), c_im.astype(_F32)
    j = jnp.arange(L + 1, dtype=_F32)[:, None, None]
    pmag = jnp.exp(lr * dt * j)
    pr, pi_ = pmag * jnp.cos(li * dt * j), pmag * jnp.sin(li * dt * j)
    abr = pr[:L, :, :, None] * bbr - pi_[:L, :, :, None] * bbi
    abi = pr[:L, :, :, None] * bbi + pi_[:L, :, :, None] * bbr
    kj = (jnp.einsum('gpn,jgnq->jgpq', cr, abr, precision=hi)
          - jnp.einsum('gpn,jgnq->jgpq', ci, abi, precision=hi))
    kj = kj.at[0].add(jnp.eye(P, dtype=_F32) * d_skip.astype(_F32).reshape(G, P)[:, :, None])
    s_idx = jnp.arange(L)[:, None]
    t_idx = jnp.arange(L)[None, :]
    lag = t_idx - s_idx
    w_t = jnp.where((lag >= 0)[:, :, None, None, None], kj[jnp.clip(lag, 0)], 0.0)
    w_t = jnp.transpose(w_t, (2, 0, 4, 1, 3)).reshape(G // 2, 2, L * P, L * P)
    ws_r = jnp.transpose(abr[::-1], (1, 0, 3, 2)).reshape(G, L * P, N)
    ws_i = jnp.transpose(abi[::-1], (1, 0, 3, 2)).reshape(G, L * P, N)
    er = cr[None] * pr[1:, :, None, :] - ci[None] * pi_[1:, :, None, :]
    ei = cr[None] * pi_[1:, :, None, :] + ci[None] * pr[1:, :, None, :]
    wg_r = jnp.transpose(er, (1, 3, 0, 2)).reshape(G, N, L * P)
    wg_i = -jnp.transpose(ei, (1, 3, 0, 2)).reshape(G, N, L * P)
    eye2 = jnp.eye(2, dtype=_F32)
    ws = jnp.stack([ws_r, ws_i], axis=0).reshape(2, G // 2, 2, L * P, N)
    w_s = jnp.einsum('rkgan,gh->kgarhn', ws, eye2).reshape(G // 2, 2 * L * P, 4 * N)
    wg = jnp.stack([wg_r, wg_i], axis=0).reshape(2, G // 2, 2, N, L * P)
    w_g = jnp.einsum('rkgna,gh->krgnha', wg, eye2).reshape(G // 2, 4 * N, 2 * L * P)
    a_r = pr[L].reshape(G // 2, 1, 2 * N)
    a_i = pi_[L].reshape(G // 2, 1, 2 * N)
    return w_t.astype(_BF16), w_s.astype(_BF16), w_g.astype(_BF16), a_r, a_i


def _ssm_kernel(u_ref, wt_ref, ws_ref, wg_ref, ar_ref, ai_ref, y_ref, v_scr, st_scr, *, bsz):
    pb, rb, _ = u_ref.shape
    half = 2 * SSM_STATE
    lp = SSM_SUB * SSM_GROUP

    @pl.when(pl.program_id(1) == 0)
    def _():
        st_scr[...] = jnp.zeros_like(st_scr)

    for k in range(pb):
        v_scr[k] = jnp.dot(u_ref[k], ws_ref[k], preferred_element_type=_F32)

    ar = [jnp.broadcast_to(ar_ref[k], (bsz, half)) for k in range(pb)]
    ai = [jnp.broadcast_to(ai_ref[k], (bsz, half)) for k in range(pb)]

    def step(c, carry):
        r0 = pl.multiple_of(c * bsz, bsz)
        out = []
        for k in range(pb):
            sr, si = carry[2 * k], carry[2 * k + 1]
            vr = v_scr[k, pl.ds(r0, bsz), 0:half]
            vi = v_scr[k, pl.ds(r0, bsz), half:2 * half]
            v_scr[k, pl.ds(r0, bsz), 0:half] = sr
            v_scr[k, pl.ds(r0, bsz), half:2 * half] = si
            out.append(ar[k] * sr - ai[k] * si + vr)
            out.append(ar[k] * si + ai[k] * sr + vi)
        return tuple(out)

    init = tuple(st_scr[i] for i in range(2 * pb))
    fin = lax.fori_loop(0, rb // bsz, step, init, unroll=4)
    for i in range(2 * pb):
        st_scr[i] = fin[i]

    for k in range(pb):
        carry_in = jnp.dot(v_scr[k].astype(_BF16), wg_ref[k], preferred_element_type=_F32)
        for gi in range(2):
            sl = slice(gi * lp, (gi + 1) * lp)
            y = jnp.dot(u_ref[k, :, sl], wt_ref[k, gi], preferred_element_type=_F32) + carry_in[:, sl]
            y_ref[k, :, sl] = y.astype(_BF16)


def _ssm(u_t, w_t, w_s, w_g, a_r, a_i, bsz, pb, rb):
    npair, rows, width = u_t.shape
    lp = SSM_SUB * SSM_GROUP
    blk = lambda shape: pl.BlockSpec(shape, lambda p, r: (p,) + (0,) * (len(shape) - 1))
    return pl.pallas_call(
        functools.partial(_ssm_kernel, bsz=bsz),
        grid=(npair // pb, rows // rb),
        in_specs=[pl.BlockSpec((pb, rb, width), lambda p, r: (p, r, 0)),
                  blk((pb, 2, lp, lp)), blk((pb, width, 4 * SSM_STATE)),
                  blk((pb, 4 * SSM_STATE, width)), blk((pb, 1, 2 * SSM_STATE)),
                  blk((pb, 1, 2 * SSM_STATE))],
        out_specs=pl.BlockSpec((pb, rb, width), lambda p, r: (p, r, 0)),
        out_shape=jax.ShapeDtypeStruct(u_t.shape, _BF16),
        scratch_shapes=[pltpu.VMEM((pb, rb, 4 * SSM_STATE), _F32),
                        pltpu.VMEM((2 * pb, bsz, 2 * SSM_STATE), _F32)],
        compiler_params=pltpu.CompilerParams(dimension_semantics=("parallel", "arbitrary"),
                                             vmem_limit_bytes=VMEM_LIMIT),
        name="ssm",
    )(u_t, w_t, w_s, w_g, a_r, a_i)


def _attn_kernel(q_ref, k_ref, v_ref, o_ref, *, tq):
    seq = q_ref.shape[2]
    nq = seq // tq
    row_chunk = lax.broadcasted_iota(jnp.int32, (tq, tq), 0) // CHUNK
    col_chunk = lax.broadcasted_iota(jnp.int32, (tq, tq), 1) // CHUNK
    diag_mask = col_chunk <= row_chunk
    nt = (((1,), (1,)), ((), ()))

    def tile(hh, q, k0, carry, mask):
        m, l, acc = carry
        k = k_ref[0, hh, pl.ds(k0, tq), :]
        s = lax.dot_general(q, k, nt, preferred_element_type=_F32)
        if mask:
            s = jnp.where(diag_mask, s, _NEG_BIG)
        m_new = jnp.maximum(m, jnp.max(s, axis=-1, keepdims=True))
        alpha = jnp.exp2(m - m_new)
        p = jnp.exp2(s - m_new)
        l = alpha * l + jnp.sum(p, axis=-1, keepdims=True)
        v = v_ref[0, pl.ds(k0, tq), hh * V_HEAD:(hh + 1) * V_HEAD]
        acc = alpha * acc + jnp.dot(p.astype(_BF16), v, preferred_element_type=_F32)
        return m_new, l, acc

    def qblock(qi, _):
        q0 = pl.multiple_of(qi * tq, tq)
        outs = []
        for hh in range(2):
            q = q_ref[0, hh, pl.ds(q0, tq), :]
            init = (jnp.full((tq, 1), _NEG_BIG, _F32), jnp.zeros((tq, 1), _F32),
                    jnp.zeros((tq, V_HEAD), _F32))
            carry = lax.fori_loop(
                0, qi, lambda kj, c: tile(hh, q, pl.multiple_of(kj * tq, tq), c, False), init)
            m, l, acc = tile(hh, q, q0, carry, True)
            outs.append(acc / l)
        o_ref[0, pl.ds(q0, tq), :] = jnp.concatenate(outs, axis=-1).astype(_BF16)
        return 0

    lax.fori_loop(0, nq, qblock, 0)


def _attn(q, k, v, tq):
    bsz, heads, seq, _ = q.shape
    qk_spec = pl.BlockSpec((1, 2, seq, HEAD_PAD), lambda b, h: (b, h, 0, 0))
    v_spec = pl.BlockSpec((1, seq, 2 * V_HEAD), lambda b, h: (b, 0, h))
    return pl.pallas_call(
        functools.partial(_attn_kernel, tq=tq),
        grid=(bsz, heads // 2),
        in_specs=[qk_spec, qk_spec, v_spec],
        out_specs=v_spec,
        out_shape=jax.ShapeDtypeStruct((bsz, seq, MLA_WIDTH), _BF16),
        compiler_params=pltpu.CompilerParams(dimension_semantics=("parallel", "parallel"),
                                             vmem_limit_bytes=VMEM_LIMIT),
        name="attn",
    )(q, k, v)


def _post_kernel(x_ref, gate_ref, y_ref, szs_ref, sgs_ref, o_ref, szm_ref, sgm_ref,
                 wglu_ref, bglu_ref, wbs_ref, wbm_ref, wout_ref, gpost_ref, out_ref):
    ge = _gelu_tanh(y_ref[0].astype(_F32)).astype(_BF16)
    gl = jnp.dot(ge, wglu_ref[...], preferred_element_type=_F32) + bglu_ref[...]
    a = (gl[:, :SSM_WIDTH] * _sigmoid(gl[:, SSM_WIDTH:])) * szs_ref[0].astype(_F32)
    ys = jnp.dot(a.astype(_BF16), wbs_ref[...], preferred_element_type=_F32)
    am = o_ref[0].astype(_F32) * szm_ref[0].astype(_F32)
    ym = jnp.dot(am.astype(_BF16), wbm_ref[...], preferred_element_type=_F32)
    merged = sgs_ref[0].astype(_F32) * ys + sgm_ref[0].astype(_F32) * ym
    out = jnp.dot(merged.astype(_BF16), wout_ref[...], preferred_element_type=_F32)
    out_ref[0] = x_ref[0] + gate_ref[0] * (_rms(out) * gpost_ref[...])


def _post(x, gate, y, szs, sgs, o, szm, sgm, wglu, bglu, wbs, wbm, wout, gpost, tm):
    bsz, seq, _ = x.shape
    tok = pl.BlockSpec((1, tm, D_MODEL), lambda b, i: (b, i, 0))
    per_b = pl.BlockSpec((1, 1, D_MODEL), lambda b, i: (b, 0, 0))
    consts = [wglu, bglu, wbs, wbm, wout, gpost]
    return pl.pallas_call(
        _post_kernel,
        grid=(bsz, seq // tm),
        in_specs=[tok, per_b] + [tok] * 6 + [_const_spec(w.shape) for w in consts],
        out_specs=tok,
        out_shape=jax.ShapeDtypeStruct(x.shape, x.dtype),
        compiler_params=pltpu.CompilerParams(dimension_semantics=("parallel", "parallel"),
                                             vmem_limit_bytes=VMEM_LIMIT),
        name="post",
    )(x, gate, y, szs, sgs, o, szm, sgm, *consts)


def _mla_weights(w_q_up, w_kv_up):
    H, half = MLA_HEADS, QK_ROPE // 2
    pad = HEAD_PAD - QK_NOPE - QK_ROPE
    wq = w_q_up.reshape(Q_LORA, H, QK_NOPE + QK_ROPE)
    q_nope, q_r1, q_r2 = wq[..., :QK_NOPE], wq[..., QK_NOPE:QK_NOPE + half], wq[..., QK_NOPE + half:]
    zq = jnp.zeros((Q_LORA, H, pad), w_q_up.dtype)
    wq_main = jnp.concatenate([q_nope, q_r1, q_r2, zq], axis=-1).reshape(Q_LORA, H * HEAD_PAD)
    wq_rot = jnp.concatenate([jnp.zeros_like(q_nope), -q_r2, q_r1, zq], axis=-1).reshape(Q_LORA, H * HEAD_PAD)
    wkv = w_kv_up.reshape(KV_LORA, H, QK_NOPE + V_HEAD)
    zk = jnp.zeros((KV_LORA, H, HEAD_PAD - QK_NOPE), w_kv_up.dtype)
    wk_main = jnp.concatenate([wkv[..., :QK_NOPE], zk], axis=-1).reshape(KV_LORA, H * HEAD_PAD)
    wv = wkv[..., QK_NOPE:].reshape(KV_LORA, H * V_HEAD)
    eye = jnp.eye(half, dtype=w_kv_up.dtype)
    z = jnp.zeros((half, half), w_kv_up.dtype)
    place = jnp.zeros((LANES, HEAD_PAD), w_kv_up.dtype)
    place_id = place.at[:QK_ROPE, QK_NOPE:QK_NOPE + QK_ROPE].set(jnp.block([[eye, z], [z, eye]]))
    place_rot = place.at[:QK_ROPE, QK_NOPE:QK_NOPE + QK_ROPE].set(jnp.block([[z, eye], [-eye, z]]))
    wkr = jnp.tile(place_id, (1, H))
    wkr_rot = jnp.tile(place_rot, (1, H))
    return [w.astype(_BF16) for w in (wq_main, wq_rot, wk_main, wkr, wkr_rot, wv)]


def _rope_inv_freq_row():
    inv_freq = ROPE_BASE ** (-jnp.arange(0, QK_ROPE, 2, dtype=_F32) / QK_ROPE)
    row = jnp.zeros((HEAD_PAD,), _F32).at[QK_NOPE:QK_NOPE + QK_ROPE].set(jnp.tile(inv_freq, 2))
    return row.reshape(1, HEAD_PAD)


def kernel(x, c, positions, w_ada, b_ada, g_pre, w_in, ssm_log_dt, ssm_a_re, ssm_a_im, ssm_b_re, ssm_b_im, ssm_c_re, ssm_c_im, ssm_d, w_glu, b_glu, g_q_norm, w_q_up, g_kv_norm, w_kv_up, w_br_ssm, w_br_mla, w_out, g_post):
    bsz, seq, _ = x.shape
    depth = w_ada.shape[0]
    assert bsz == SUBLANES and seq % 256 == 0
    tm = min(256, seq)
    tq = 256
    nsub = seq // SSM_SUB
    rows = nsub * bsz
    rb = min(1024, rows)
    pb = 4
    pos3 = positions.reshape(bsz, seq, 1)
    invf = _rope_inv_freq_row()
    off = np.cumsum((0, SSM_WIDTH, SSM_WIDTH, Q_LORA, KV_LORA, QK_ROPE, MLA_WIDTH, D_MODEL, D_MODEL))

    for l in range(depth):
        mod = _ada(c, w_ada[l], b_ada[l])
        shift, scale, gate = (mod[:, None, i * D_MODEL:(i + 1) * D_MODEL] for i in range(3))

        wb = w_in[l].astype(_BF16)
        seg = lambda i: wb[:, off[i]:off[i + 1]]
        w_lat = jnp.concatenate(
            [seg(2), seg(3), seg(4), jnp.zeros((D_MODEL, LAT_PAD - Q_LORA - KV_LORA - QK_ROPE), _BF16)], axis=1)
        ws = [seg(0), seg(1), seg(5), seg(6), seg(7), w_lat,
              g_q_norm[l].reshape(1, Q_LORA), g_kv_norm[l].reshape(1, KV_LORA)]
        ws += _mla_weights(w_q_up[l], w_kv_up[l])
        u, szs, szm, sgs, sgm, q, k, v = _inproj(
            x, shift, scale, g_pre[l].reshape(1, D_MODEL), pos3, invf, ws, tm)

        u_t = u.reshape(bsz, nsub, SSM_SUB, SSM_PAIRS, 2, SSM_GROUP)
        u_t = jnp.transpose(u_t, (3, 1, 0, 4, 2, 5)).reshape(SSM_PAIRS, rows, 2 * SSM_SUB * SSM_GROUP)
        w_t, w_s, w_g, a_r, a_i = _ssm_weights(
            ssm_log_dt[l], ssm_a_re[l], ssm_a_im[l], ssm_b_re[l], ssm_b_im[l],
            ssm_c_re[l], ssm_c_im[l], ssm_d[l])
        y_t = _ssm(u_t, w_t, w_s, w_g, a_r, a_i, bsz, pb, rb)
        y = y_t.reshape(SSM_PAIRS, nsub, bsz, 2, SSM_SUB, SSM_GROUP)
        y = jnp.transpose(y, (2, 1, 4, 0, 3, 5)).reshape(bsz, seq, SSM_WIDTH)

        o = _attn(q, k, v, tq)

        x = _post(x, gate, y, szs, sgs, o, szm, sgm,
                  w_glu[l].astype(_BF16), b_glu[l].reshape(1, -1), w_br_ssm[l].astype(_BF16),
                  w_br_mla[l].astype(_BF16), w_out[l].astype(_BF16), g_post[l].reshape(1, D_MODEL), tm)
    return x
```

```python
import functools
import math

import jax
import jax.numpy as jnp
import numpy as np
from jax import lax
from jax.experimental import pallas as pl
from jax.experimental.pallas import tpu as pltpu

D_MODEL = 1024
CHUNK = 64
SSM_WIDTH = 1024
SSM_GROUP = 16
SSM_GROUPS = SSM_WIDTH // SSM_GROUP
SSM_STATE = 64
MLA_HEADS = 16
QK_NOPE = 64
QK_ROPE = 32
V_HEAD = 64
Q_LORA = 256
KV_LORA = 256
MLA_WIDTH = MLA_HEADS * V_HEAD
ROPE_BASE = 10000.0
EPS = 1e-6

LANES = 128
SUBLANES = 8
HEAD_PAD = 128
LAT_PAD = 640
SSM_SUB = 8
SSM_BUNDLE = LANES // SSM_GROUP
SSM_NB = SSM_GROUPS // SSM_BUNDLE
SSM_FLAT = SSM_SUB * LANES
SSM_NS = SSM_BUNDLE * SSM_STATE
ATT_TILE = 256
ATT_HEADS = 4
ONES_ROWS = 16
VMEM_LIMIT = 56 * 1024 * 1024

_F32 = jnp.float32
_BF16 = jnp.bfloat16
_SOFTMAX_SCALE_LOG2E = ((QK_NOPE + QK_ROPE) ** -0.5) * math.log2(math.e)
_NEG_BIG = -1e30


def _const_spec(shape):
    zeros = (0,) * len(shape)
    return pl.BlockSpec(shape, lambda *_: zeros, pipeline_mode=pl.Buffered(1))


def _sigmoid(x):
    return 1.0 / (1.0 + jnp.exp(-x))


def _gelu_tanh(x):
    return 0.5 * x * (1.0 + jnp.tanh(math.sqrt(2.0 / math.pi) * (x + 0.044715 * (x * x * x))))


def _rms(x):
    return x * lax.rsqrt(jnp.mean(x * x, axis=-1, keepdims=True) + EPS)


def _ada_kernel(c_ref, w_ref, b_ref, o_ref):
    o_ref[...] = jnp.dot(c_ref[...], w_ref[...], preferred_element_type=_F32) + b_ref[...]


def _ada(c, w, b):
    bsz = c.shape[0]
    n = w.shape[1]
    bn = D_MODEL
    return pl.pallas_call(
        _ada_kernel,
        grid=(n // bn,),
        in_specs=[pl.BlockSpec((bsz, D_MODEL), lambda j: (0, 0)),
                  pl.BlockSpec((D_MODEL, bn), lambda j: (0, j)),
                  pl.BlockSpec((1, bn), lambda j: (0, j))],
        out_specs=pl.BlockSpec((bsz, bn), lambda j: (0, j)),
        out_shape=jax.ShapeDtypeStruct((bsz, n), _F32),
        compiler_params=pltpu.CompilerParams(dimension_semantics=("arbitrary",),
                                             vmem_limit_bytes=VMEM_LIMIT),
        name="ada",
    )(c, w, b.reshape(1, n))


def _inproj_kernel(x_ref, shift_ref, scale_ref, gpre_ref, pos_ref, invf_ref,
                   w_u_ref, w_zs_ref, w_zm_ref, w_gs_ref, w_gm_ref, w_lat_ref,
                   gq_ref, gkv_ref, wq_ref, wqr_ref, wk_ref, wkr_ref, wkrr_ref, wv_ref,
                   u_ref, szs_ref, szm_ref, sgs_ref, sgm_ref, q_ref, k_ref, vt_ref):
    x = x_ref[0]
    h = _rms(x) * gpre_ref[...] * (1.0 + scale_ref[0]) + shift_ref[0]
    hb = h.astype(_BF16)

    def proj(w_ref):
        return jnp.dot(hb, w_ref[...], preferred_element_type=_F32)

    u_ref[0] = proj(w_u_ref)
    z = proj(w_zs_ref)
    szs_ref[0] = (z * _sigmoid(z)).astype(_BF16)
    z = proj(w_zm_ref)
    szm_ref[0] = (z * _sigmoid(z)).astype(_BF16)
    sgs_ref[0] = _sigmoid(proj(w_gs_ref)).astype(_BF16)
    sgm_ref[0] = _sigmoid(proj(w_gm_ref)).astype(_BF16)

    lat = proj(w_lat_ref)
    qn = (_rms(lat[:, :Q_LORA]) * gq_ref[...]).astype(_BF16)
    kvn = (_rms(lat[:, Q_LORA:Q_LORA + KV_LORA]) * gkv_ref[...]).astype(_BF16)
    kr = lat[:, Q_LORA + KV_LORA:].astype(_BF16)

    ang = pos_ref[0].astype(_F32) * invf_ref[...]
    cos = jnp.cos(ang)
    sin = jnp.sin(ang)

    qa = jnp.dot(qn, wq_ref[...], preferred_element_type=_F32)
    qb = jnp.dot(qn, wqr_ref[...], preferred_element_type=_F32)
    ka = (jnp.dot(kvn, wk_ref[...], preferred_element_type=_F32)
          + jnp.dot(kr, wkr_ref[...], preferred_element_type=_F32))
    kb = jnp.dot(kr, wkrr_ref[...], preferred_element_type=_F32)
    for hd in range(MLA_HEADS):
        sl = slice(hd * HEAD_PAD, (hd + 1) * HEAD_PAD)
        q_ref[0, hd] = ((qa[:, sl] * cos + qb[:, sl] * sin) * _SOFTMAX_SCALE_LOG2E).astype(_BF16)
        k_ref[0, hd] = (ka[:, sl] * cos + kb[:, sl] * sin).astype(_BF16)
    vt_ref[0, 0] = lax.dot_general(wv_ref[...], kvn, (((1,), (1,)), ((), ())),
                                   preferred_element_type=_F32).astype(_BF16)


def _inproj(x, shift, scale, g_pre, pos3, invf, ws, tm):
    bsz, seq, _ = x.shape
    tok = lambda w: pl.BlockSpec((1, tm, w), lambda b, i: (b, i, 0))
    per_b = pl.BlockSpec((1, 1, D_MODEL), lambda b, i: (b, 0, 0))
    head = pl.BlockSpec((1, MLA_HEADS, tm, HEAD_PAD), lambda b, i: (b, 0, i, 0))
    vt_spec = pl.BlockSpec((1, 1, MLA_WIDTH, tm), lambda b, i: (b, i, 0, 0))
    act = jax.ShapeDtypeStruct((bsz, seq, D_MODEL), _BF16)
    hshape = jax.ShapeDtypeStruct((bsz, MLA_HEADS, seq, HEAD_PAD), _BF16)
    w_specs = [_const_spec(w.shape) for w in ws]
    return pl.pallas_call(
        _inproj_kernel,
        grid=(bsz, seq // tm),
        in_specs=[tok(D_MODEL), per_b, per_b, _const_spec((1, D_MODEL)), tok(1),
                  _const_spec((1, HEAD_PAD))] + w_specs,
        out_specs=[tok(D_MODEL)] * 5 + [head, head, vt_spec],
        out_shape=[jax.ShapeDtypeStruct((bsz, seq, SSM_WIDTH), _F32)] + [act] * 4
                  + [hshape, hshape, jax.ShapeDtypeStruct((bsz, seq // tm, MLA_WIDTH, tm), _BF16)],
        compiler_params=pltpu.CompilerParams(dimension_semantics=("parallel", "parallel"),
                                             vmem_limit_bytes=VMEM_LIMIT),
        name="inproj",
    )(x, shift, scale, g_pre, pos3, invf, *ws)


def _ssm_weights(log_dt, a_re, a_im, b_re, b_im, c_re, c_im, d_skip):
    hi = lax.Precision.HIGHEST
    L, G, N, P = SSM_SUB, SSM_GROUPS, SSM_STATE, SSM_GROUP
    nb, gb = SSM_NB, SSM_BUNDLE
    dt = jnp.exp(log_dt.astype(_F32))[:, None]
    lr, li = a_re.astype(_F32), a_im.astype(_F32)
    mag = jnp.exp(lr * dt)
    abar_re, abar_im = mag * jnp.cos(li * dt), mag * jnp.sin(li * dt)
    den = lr * lr + li * li
    nr, ni = abar_re - 1.0, abar_im
    fr = (nr * lr + ni * li) / den
    fi = (ni * lr - nr * li) / den
    br, bi = b_re.astype(_F32), b_im.astype(_F32)
    bbr = fr[..., None] * br - fi[..., None] * bi
    bbi = fr[..., None] * bi + fi[..., None] * br
    cr, ci = c_re.astype(_F32), c_im.astype(_F32)
    j = jnp.arange(L + 1, dtype=_F32)[:, None, None]
    pmag = jnp.exp(lr * dt * j)
    pr, pi_ = pmag * jnp.cos(li * dt * j), pmag * jnp.sin(li * dt * j)
    abr = pr[:L, :, :, None] * bbr - pi_[:L, :, :, None] * bbi
    abi = pr[:L, :, :, None] * bbi + pi_[:L, :, :, None] * bbr
    kj = (jnp.einsum('gpn,jgnq->jgpq', cr, abr, precision=hi)
          - jnp.einsum('gpn,jgnq->jgpq', ci, abi, precision=hi))
    kj = kj.at[0].add(jnp.eye(P, dtype=_F32) * d_skip.astype(_F32).reshape(G, P)[:, :, None])
    lag = jnp.arange(L)[None, :] - jnp.arange(L)[:, None]
    toe = jnp.where((lag >= 0)[:, :, None, None, None], kj[jnp.clip(lag, 0)], 0.0)
    eye = jnp.eye(gb, dtype=_F32)
    toe = toe.reshape(L, L, nb, gb, P, P)
    w_t = jnp.einsum('stbgpq,gh->bsgqthp', toe, eye).reshape(nb, SSM_FLAT, SSM_FLAT)
    ab = jnp.stack([abr[::-1], abi[::-1]], axis=0).reshape(2, L, nb, gb, N, P)
    w_s = jnp.einsum('rsbgnq,gh->bsgqrhn', ab, eye).reshape(nb, SSM_FLAT, 2 * SSM_NS)
    er = cr[None] * pr[1:, :, None, :] - ci[None] * pi_[1:, :, None, :]
    ei = cr[None] * pi_[1:, :, None, :] + ci[None] * pr[1:, :, None, :]
    e = jnp.stack([er, -ei], axis=0).reshape(2, L, nb, gb, P, N)
    w_g = jnp.einsum('rtbgpn,gh->brgnthp', e, eye).reshape(nb, 2 * SSM_NS, SSM_FLAT)
    a_r = pr[L].reshape(nb, 1, SSM_NS)
    a_i = pi_[L].reshape(nb, 1, SSM_NS)
    return w_t.astype(_BF16), w_s.astype(_BF16), w_g.astype(_BF16), a_r, a_i


def _ssm_kernel(u_ref, wt_ref, ws_ref, wg_ref, ar_ref, ai_ref, y_ref, lhs_scr, v_scr, yp_scr, st_scr):
    bsz, sblk, _ = u_ref.shape
    nsub = sblk // SSM_SUB
    ns = SSM_NS

    @pl.when(pl.program_id(1) == 0)
    def _():
        st_scr[...] = jnp.zeros_like(st_scr)

    for b in range(bsz):
        for s in range(SSM_SUB):
            lhs_scr[s, pl.ds(b, nsub, stride=bsz), :] = u_ref[b, pl.ds(s, nsub, stride=SSM_SUB), :]
    lhs = jnp.concatenate([lhs_scr[s] for s in range(SSM_SUB)], axis=-1).astype(_BF16)
    v_scr[...] = jnp.dot(lhs, ws_ref[0], preferred_element_type=_F32)

    ar = jnp.broadcast_to(ar_ref[0], (bsz, ns))
    ai = jnp.broadcast_to(ai_ref[0], (bsz, ns))

    def step(c, carry):
        sr, si = carry
        r0 = pl.multiple_of(c * bsz, bsz)
        vr = v_scr[pl.ds(r0, bsz), 0:ns]
        vi = v_scr[pl.ds(r0, bsz), ns:2 * ns]
        v_scr[pl.ds(r0, bsz), 0:ns] = sr
        v_scr[pl.ds(r0, bsz), ns:2 * ns] = si
        return ar * sr - ai * si + vr, ar * si + ai * sr + vi

    sr, si = lax.fori_loop(0, nsub, step, (st_scr[0], st_scr[1]), unroll=4)
    st_scr[0] = sr
    st_scr[1] = si

    yp = (jnp.dot(lhs, wt_ref[0], preferred_element_type=_F32)
          + jnp.dot(v_scr[...].astype(_BF16), wg_ref[0], preferred_element_type=_F32))
    for t in range(SSM_SUB):
        yp_scr[t] = yp[:, t * LANES:(t + 1) * LANES]
    for b in range(bsz):
        for t in range(SSM_SUB):
            y_ref[b, pl.ds(t, nsub, stride=SSM_SUB), :] = yp_scr[t, pl.ds(b, nsub, stride=bsz), :]


def _ssm(u, w_t, w_s, w_g, a_r, a_i, sblk):
    bsz, seq, _ = u.shape
    rows = (sblk // SSM_SUB) * bsz
    tok = pl.BlockSpec((bsz, sblk, LANES), lambda g, i: (0, i, g))
    wspec = lambda shape: pl.BlockSpec((1,) + shape, lambda g, i: (g, 0, 0))
    return pl.pallas_call(
        _ssm_kernel,
        grid=(SSM_NB, seq // sblk),
        in_specs=[tok, wspec((SSM_FLAT, SSM_FLAT)), wspec((SSM_FLAT, 2 * SSM_NS)),
                  wspec((2 * SSM_NS, SSM_FLAT)), wspec((1, SSM_NS)), wspec((1, SSM_NS))],
        out_specs=tok,
        out_shape=jax.ShapeDtypeStruct(u.shape, _F32),
        scratch_shapes=[pltpu.VMEM((SSM_SUB, rows, LANES), _F32),
                        pltpu.VMEM((rows, 2 * SSM_NS), _F32),
                        pltpu.VMEM((SSM_SUB, rows, LANES), _F32),
                        pltpu.VMEM((2, bsz, SSM_NS), _F32)],
        compiler_params=pltpu.CompilerParams(dimension_semantics=("parallel", "arbitrary"),
                                             vmem_limit_bytes=VMEM_LIMIT),
        name="ssm",
    )(u, w_t, w_s, w_g, a_r, a_i)


def _attn_kernel(q_ref, k_ref, vt_ref, o_ref, sa_ref, sb_ref, acc_ref, bias_ref):
    t = ATT_TILE
    nh = q_ref.shape[1]
    seq = q_ref.shape[2]
    key_chunk = lax.broadcasted_iota(jnp.int32, (t, t), 0) // CHUNK
    qry_chunk = lax.broadcasted_iota(jnp.int32, (t, t), 1) // CHUNK
    diag_mask = key_chunk <= qry_chunk
    ones = jnp.ones((ONES_ROWS, t), _BF16)
    nt = (((1,), (1,)), ((), ()))

    nq = seq // t
    bias_ref[0] = jnp.zeros((t, t), _F32)
    bias_ref[1] = jnp.where(diag_mask, 0.0, _NEG_BIG)

    def scores(dst_ref, qi, kj):
        q0 = pl.multiple_of(qi * t, t)
        k0 = pl.multiple_of(kj * t, t)
        for hh in range(nh):
            dst_ref[hh] = lax.dot_general(k_ref[0, hh, pl.ds(k0, t), :], q_ref[0, hh, pl.ds(q0, t), :],
                                          nt, preferred_element_type=_F32)

    def step(cur_ref, nxt_ref, carry):
        qi, kj, ms, accs = carry
        last = kj == qi
        kj_n = jnp.where(last, 0, kj + 1)
        qi_n = jnp.where(last, qi + 1, qi)
        scores(nxt_ref, jnp.minimum(qi_n, nq - 1), kj_n)
        bias = bias_ref[last.astype(jnp.int32)]
        ms_n, accs_n = [], []
        for hh in range(nh):
            st = cur_ref[hh] + bias
            m_new = jnp.maximum(ms[hh], jnp.max(st, axis=0, keepdims=True))
            alpha = jnp.exp2(ms[hh] - m_new)
            p = jnp.exp2(st - m_new).astype(_BF16)
            vte = jnp.concatenate([vt_ref[0, kj, hh * V_HEAD:(hh + 1) * V_HEAD, :], ones], axis=0)
            acc = alpha * accs[hh] + jnp.dot(vte, p, preferred_element_type=_F32)
            acc_ref[qi, hh] = acc
            ms_n.append(jnp.where(last, _NEG_BIG, m_new))
            accs_n.append(jnp.where(last, 0.0, acc))
        return qi_n, kj_n, tuple(ms_n), tuple(accs_n)

    scores(sa_ref, 0, 0)
    init = (jnp.int32(0), jnp.int32(0),
            tuple(jnp.full((1, t), _NEG_BIG, _F32) for _ in range(nh)),
            tuple(jnp.zeros((V_HEAD + ONES_ROWS, t), _F32) for _ in range(nh)))
    nsteps = nq * (nq + 1) // 2
    carry = lax.fori_loop(0, nsteps // 2,
                          lambda _, c: step(sb_ref, sa_ref, step(sa_ref, sb_ref, c)), init)
    if nsteps % 2:
        step(sa_ref, sb_ref, carry)

    def finish(qi, _):
        outs = [acc_ref[qi, hh, :V_HEAD] / acc_ref[qi, hh, V_HEAD:V_HEAD + 1] for hh in range(nh)]
        o_ref[0, pl.ds(pl.multiple_of(qi * t, t), t), :] = (
            jnp.transpose(jnp.concatenate(outs, axis=0)).astype(_BF16))
        return 0

    lax.fori_loop(0, nq, finish, 0)


def _attn(q, k, vt):
    bsz, heads, seq, _ = q.shape
    nh = ATT_HEADS
    qk_spec = pl.BlockSpec((1, nh, seq, HEAD_PAD), lambda b, h: (b, h, 0, 0))
    vt_spec = pl.BlockSpec((1, seq // ATT_TILE, nh * V_HEAD, ATT_TILE), lambda b, h: (b, 0, h, 0))
    return pl.pallas_call(
        _attn_kernel,
        grid=(bsz, heads // nh),
        in_specs=[qk_spec, qk_spec, vt_spec],
        out_specs=pl.BlockSpec((1, seq, nh * V_HEAD), lambda b, h: (b, 0, h)),
        out_shape=jax.ShapeDtypeStruct((bsz, seq, MLA_WIDTH), _BF16),
        scratch_shapes=[pltpu.VMEM((nh, ATT_TILE, ATT_TILE), _F32),
                        pltpu.VMEM((nh, ATT_TILE, ATT_TILE), _F32),
                        pltpu.VMEM((seq // ATT_TILE, nh, V_HEAD + ONES_ROWS, ATT_TILE), _F32),
                        pltpu.VMEM((2, ATT_TILE, ATT_TILE), _F32)],
        compiler_params=pltpu.CompilerParams(dimension_semantics=("parallel", "parallel"),
                                             vmem_limit_bytes=VMEM_LIMIT),
        name="attn",
    )(q, k, vt)


def _post_kernel(x_ref, gate_ref, y_ref, szs_ref, sgs_ref, o_ref, szm_ref, sgm_ref,
                 wglu_ref, bglu_ref, wbs_ref, wbm_ref, wout_ref, gpost_ref, out_ref):
    ge = _gelu_tanh(y_ref[0]).astype(_BF16)
    gl = jnp.dot(ge, wglu_ref[...], preferred_element_type=_F32) + bglu_ref[...]
    a = (gl[:, :SSM_WIDTH] * _sigmoid(gl[:, SSM_WIDTH:])) * szs_ref[0].astype(_F32)
    ys = jnp.dot(a.astype(_BF16), wbs_ref[...], preferred_element_type=_F32)
    am = o_ref[0].astype(_F32) * szm_ref[0].astype(_F32)
    ym = jnp.dot(am.astype(_BF16), wbm_ref[...], preferred_element_type=_F32)
    merged = sgs_ref[0].astype(_F32) * ys + sgm_ref[0].astype(_F32) * ym
    out = jnp.dot(merged.astype(_BF16), wout_ref[...], preferred_element_type=_F32)
    out_ref[0] = x_ref[0] + gate_ref[0] * (_rms(out) * gpost_ref[...])


def _post(x, gate, y, szs, sgs, o, szm, sgm, wglu, bglu, wbs, wbm, wout, gpost, tm):
    bsz, seq, _ = x.shape
    tok = pl.BlockSpec((1, tm, D_MODEL), lambda b, i: (b, i, 0))
    per_b = pl.BlockSpec((1, 1, D_MODEL), lambda b, i: (b, 0, 0))
    consts = [wglu, bglu, wbs, wbm, wout, gpost]
    return pl.pallas_call(
        _post_kernel,
        grid=(bsz, seq // tm),
        in_specs=[tok, per_b] + [tok] * 6 + [_const_spec(w.shape) for w in consts],
        out_specs=tok,
        out_shape=jax.ShapeDtypeStruct(x.shape, x.dtype),
        compiler_params=pltpu.CompilerParams(dimension_semantics=("parallel", "parallel"),
                                             vmem_limit_bytes=VMEM_LIMIT),
        name="post",
    )(x, gate, y, szs, sgs, o, szm, sgm, *consts)


def _mla_weights(w_q_up, w_kv_up):
    H, half = MLA_HEADS, QK_ROPE // 2
    pad = HEAD_PAD - QK_NOPE - QK_ROPE
    wq = w_q_up.reshape(Q_LORA, H, QK_NOPE + QK_ROPE)
    q_nope, q_r1, q_r2 = wq[..., :QK_NOPE], wq[..., QK_NOPE:QK_NOPE + half], wq[..., QK_NOPE + half:]
    zq = jnp.zeros((Q_LORA, H, pad), w_q_up.dtype)
    wq_main = jnp.concatenate([q_nope, q_r1, q_r2, zq], axis=-1).reshape(Q_LORA, H * HEAD_PAD)
    wq_rot = jnp.concatenate([jnp.zeros_like(q_nope), -q_r2, q_r1, zq], axis=-1).reshape(Q_LORA, H * HEAD_PAD)
    wkv = w_kv_up.reshape(KV_LORA, H, QK_NOPE + V_HEAD)
    zk = jnp.zeros((KV_LORA, H, HEAD_PAD - QK_NOPE), w_kv_up.dtype)
    wk_main = jnp.concatenate([wkv[..., :QK_NOPE], zk], axis=-1).reshape(KV_LORA, H * HEAD_PAD)
    wv = jnp.transpose(wkv[..., QK_NOPE:].reshape(KV_LORA, H * V_HEAD))
    eye = jnp.eye(half, dtype=w_kv_up.dtype)
    z = jnp.zeros((half, half), w_kv_up.dtype)
    place = jnp.zeros((LANES, HEAD_PAD), w_kv_up.dtype)
    place_id = place.at[:QK_ROPE, QK_NOPE:QK_NOPE + QK_ROPE].set(jnp.block([[eye, z], [z, eye]]))
    place_rot = place.at[:QK_ROPE, QK_NOPE:QK_NOPE + QK_ROPE].set(jnp.block([[z, eye], [-eye, z]]))
    wkr = jnp.tile(place_id, (1, H))
    wkr_rot = jnp.tile(place_rot, (1, H))
    return [w.astype(_BF16) for w in (wq_main, wq_rot, wk_main, wkr, wkr_rot, wv)]


def _rope_inv_freq_row():
    inv_freq = ROPE_BASE ** (-jnp.arange(0, QK_ROPE, 2, dtype=_F32) / QK_ROPE)
    row = jnp.zeros((HEAD_PAD,), _F32).at[QK_NOPE:QK_NOPE + QK_ROPE].set(jnp.tile(inv_freq, 2))
    return row.reshape(1, HEAD_PAD)


def kernel(x, c, positions, w_ada, b_ada, g_pre, w_in, ssm_log_dt, ssm_a_re, ssm_a_im, ssm_b_re, ssm_b_im, ssm_c_re, ssm_c_im, ssm_d, w_glu, b_glu, g_q_norm, w_q_up, g_kv_norm, w_kv_up, w_br_ssm, w_br_mla, w_out, g_post):
    bsz, seq, _ = x.shape
    depth = w_ada.shape[0]
    assert bsz == SUBLANES and seq % ATT_TILE == 0
    tm = ATT_TILE
    sblk = min(512, seq)
    pos3 = positions.reshape(bsz, seq, 1)
    invf = _rope_inv_freq_row()
    off = np.cumsum((0, SSM_WIDTH, SSM_WIDTH, Q_LORA, KV_LORA, QK_ROPE, MLA_WIDTH, D_MODEL, D_MODEL))

    for l in range(depth):
        mod = _ada(c, w_ada[l], b_ada[l])
        shift, scale, gate = (mod[:, None, i * D_MODEL:(i + 1) * D_MODEL] for i in range(3))

        wb = w_in[l].astype(_BF16)
        seg = lambda i: wb[:, off[i]:off[i + 1]]
        w_lat = jnp.concatenate(
            [seg(2), seg(3), seg(4), jnp.zeros((D_MODEL, LAT_PAD - Q_LORA - KV_LORA - QK_ROPE), _BF16)], axis=1)
        ws = [seg(0), seg(1), seg(5), seg(6), seg(7), w_lat,
              g_q_norm[l].reshape(1, Q_LORA), g_kv_norm[l].reshape(1, KV_LORA)]
        ws += _mla_weights(w_q_up[l], w_kv_up[l])
        u, szs, szm, sgs, sgm, q, k, vt = _inproj(
            x, shift, scale, g_pre[l].reshape(1, D_MODEL), pos3, invf, ws, tm)

        w_t, w_s, w_g, a_r, a_i = _ssm_weights(
            ssm_log_dt[l], ssm_a_re[l], ssm_a_im[l], ssm_b_re[l], ssm_b_im[l],
            ssm_c_re[l], ssm_c_im[l], ssm_d[l])
        y = _ssm(u, w_t, w_s, w_g, a_r, a_i, sblk)

        o = _attn(q, k, vt)

        x = _post(x, gate, y, szs, sgs, o, szm, sgm,
                  w_glu[l].astype(_BF16), b_glu[l].reshape(1, -1), w_br_ssm[l].astype(_BF16),
                  w_br_mla[l].astype(_BF16), w_out[l].astype(_BF16), g_post[l].reshape(1, D_MODEL), tm)
    return x
```

```python
import functools
import math

import jax
import jax.numpy as jnp
import numpy as np
from jax import lax
from jax.experimental import pallas as pl
from jax.experimental.pallas import tpu as pltpu

D_MODEL = 1024
CHUNK = 64
SSM_WIDTH = 1024
SSM_GROUP = 16
SSM_GROUPS = SSM_WIDTH // SSM_GROUP
SSM_STATE = 64
MLA_HEADS = 16
QK_NOPE = 64
QK_ROPE = 32
V_HEAD = 64
Q_LORA = 256
KV_LORA = 256
MLA_WIDTH = MLA_HEADS * V_HEAD
ROPE_BASE = 10000.0
EPS = 1e-6

LANES = 128
SUBLANES = 8
HEAD_PAD = 128
SSM_SUB = 8
SSM_BUNDLE = LANES // SSM_GROUP
SSM_NB = SSM_GROUPS // SSM_BUNDLE
SSM_FLAT = SSM_SUB * LANES
SSM_NS = SSM_BUNDLE * SSM_STATE
ATT_TILE = 256
ATT_HEADS = 4
ONES_ROWS = 16
VMEM_LIMIT = 56 * 1024 * 1024

_F32 = jnp.float32
_BF16 = jnp.bfloat16
_SOFTMAX_SCALE_LOG2E = ((QK_NOPE + QK_ROPE) ** -0.5) * math.log2(math.e)
_NEG_BIG = -1e30


def _const_spec(shape):
    zeros = (0,) * len(shape)
    return pl.BlockSpec(shape, lambda *_: zeros, pipeline_mode=pl.Buffered(1))


def _sigmoid(x):
    return 1.0 / (1.0 + jnp.exp(-x))


def _gelu_tanh(x):
    return 0.5 * x * (1.0 + jnp.tanh(math.sqrt(2.0 / math.pi) * (x + 0.044715 * (x * x * x))))


def _rms(x):
    return x * lax.rsqrt(jnp.mean(x * x, axis=-1, keepdims=True) + EPS)


def _ada_kernel(c_ref, w_ref, b_ref, o_ref):
    o_ref[...] = jnp.dot(c_ref[...], w_ref[...], preferred_element_type=_F32) + b_ref[...]


def _ada(c, w, b):
    bsz = c.shape[0]
    n = w.shape[1]
    bn = D_MODEL
    return pl.pallas_call(
        _ada_kernel,
        grid=(n // bn,),
        in_specs=[pl.BlockSpec((bsz, D_MODEL), lambda j: (0, 0)),
                  pl.BlockSpec((D_MODEL, bn), lambda j: (0, j)),
                  pl.BlockSpec((1, bn), lambda j: (0, j))],
        out_specs=pl.BlockSpec((bsz, bn), lambda j: (0, j)),
        out_shape=jax.ShapeDtypeStruct((bsz, n), _F32),
        compiler_params=pltpu.CompilerParams(dimension_semantics=("arbitrary",),
                                             vmem_limit_bytes=VMEM_LIMIT),
        name="ada",
    )(c, w, b.reshape(1, n))


def _inproj_kernel(x_ref, shift_ref, scale_ref, gpre_ref, pos_ref, invf_ref,
                   w_u_ref, w_zs_ref, w_zm_ref, w_gs_ref, w_gm_ref, w_lat_ref, w_krt_ref,
                   gq_ref, gkv_ref, wqt_ref, wk_ref, wvt_ref,
                   u_ref, szs_ref, szm_ref, sgs_ref, sgm_ref, qt_ref, k_ref, vt_ref):
    tm = x_ref.shape[1]
    half = QK_ROPE // 2
    r1 = slice(QK_NOPE, QK_NOPE + half)
    r2 = slice(QK_NOPE + half, QK_NOPE + QK_ROPE)
    nt = (((1,), (1,)), ((), ()))
    x = x_ref[0]
    h = _rms(x) * gpre_ref[...] * (1.0 + scale_ref[0]) + shift_ref[0]
    hb = h.astype(_BF16)

    def proj(w_ref):
        return jnp.dot(hb, w_ref[...], preferred_element_type=_F32)

    u_ref[0] = proj(w_u_ref)
    z = proj(w_zs_ref)
    szs_ref[0] = (z * _sigmoid(z)).astype(_BF16)
    z = proj(w_zm_ref)
    szm_ref[0] = (z * _sigmoid(z)).astype(_BF16)
    sgs_ref[0] = _sigmoid(proj(w_gs_ref)).astype(_BF16)
    sgm_ref[0] = _sigmoid(proj(w_gm_ref)).astype(_BF16)

    lat = proj(w_lat_ref)
    qn = (_rms(lat[:, :Q_LORA]) * gq_ref[...]).astype(_BF16)
    kvn = (_rms(lat[:, Q_LORA:]) * gkv_ref[...]).astype(_BF16)

    ang = invf_ref[...] * pos_ref[0].astype(_F32)
    cos = jnp.cos(ang)
    sin = jnp.sin(ang)

    qt = lax.dot_general(wqt_ref[...], qn, nt, preferred_element_type=_F32)
    for hd in range(MLA_HEADS):
        blk = qt[hd * HEAD_PAD:(hd + 1) * HEAD_PAD]
        t1, t2 = blk[r1], blk[r2]
        blk = jnp.concatenate([blk[:QK_NOPE], t1 * cos - t2 * sin, t2 * cos + t1 * sin,
                               blk[QK_NOPE + QK_ROPE:]], axis=0)
        qt_ref[0, 0, hd * HEAD_PAD:(hd + 1) * HEAD_PAD, :] = (blk * _SOFTMAX_SCALE_LOG2E).astype(_BF16)

    krt = lax.dot_general(w_krt_ref[...], hb, nt, preferred_element_type=_F32)
    t1, t2 = krt[:half], krt[half:QK_ROPE]
    spare = HEAD_PAD - QK_NOPE - QK_ROPE
    row = lax.broadcasted_iota(jnp.int32, (spare, tm), 0)
    col_chunk = lax.broadcasted_iota(jnp.int32, (spare, tm), 1) // CHUNK
    onehot = jnp.where(row == col_chunk, 1.0, 0.0)
    kr_full = jnp.concatenate([jnp.zeros((QK_NOPE, tm), _F32), t1 * cos - t2 * sin,
                               t2 * cos + t1 * sin, onehot], axis=0)
    kr_place = jnp.transpose(kr_full)
    ka = jnp.dot(kvn, wk_ref[...], preferred_element_type=_F32)
    for hd in range(MLA_HEADS):
        k_ref[0, hd] = (ka[:, hd * HEAD_PAD:(hd + 1) * HEAD_PAD] + kr_place).astype(_BF16)
    vt_ref[0, 0] = lax.dot_general(wvt_ref[...], kvn, nt, preferred_element_type=_F32).astype(_BF16)


def _inproj(x, shift, scale, g_pre, pos3, invf, ws, tm):
    bsz, seq, _ = x.shape
    tok = lambda w: pl.BlockSpec((1, tm, w), lambda b, i: (b, i, 0))
    per_b = pl.BlockSpec((1, 1, D_MODEL), lambda b, i: (b, 0, 0))
    head = pl.BlockSpec((1, MLA_HEADS, tm, HEAD_PAD), lambda b, i: (b, 0, i, 0))
    tiled = lambda rows: pl.BlockSpec((1, 1, rows, tm), lambda b, i: (b, i, 0, 0))
    tshape = lambda rows: jax.ShapeDtypeStruct((bsz, seq // tm, rows, tm), _BF16)
    act = jax.ShapeDtypeStruct((bsz, seq, D_MODEL), _BF16)
    w_specs = [_const_spec(w.shape) for w in ws]
    return pl.pallas_call(
        _inproj_kernel,
        grid=(bsz, seq // tm),
        in_specs=[tok(D_MODEL), per_b, per_b, _const_spec((1, D_MODEL)),
                  pl.BlockSpec((1, 1, tm), lambda b, i: (b, 0, i)),
                  _const_spec((QK_ROPE // 2, 1))] + w_specs,
        out_specs=[tok(D_MODEL)] * 5 + [tiled(MLA_HEADS * HEAD_PAD), head, tiled(MLA_WIDTH)],
        out_shape=[jax.ShapeDtypeStruct((bsz, seq, SSM_WIDTH), _F32)] + [act] * 4
                  + [tshape(MLA_HEADS * HEAD_PAD),
                     jax.ShapeDtypeStruct((bsz, MLA_HEADS, seq, HEAD_PAD), _BF16), tshape(MLA_WIDTH)],
        compiler_params=pltpu.CompilerParams(dimension_semantics=("parallel", "parallel"),
                                             vmem_limit_bytes=VMEM_LIMIT),
        name="inproj",
    )(x, shift, scale, g_pre, pos3, invf, *ws)


def _ssm_weights(log_dt, a_re, a_im, b_re, b_im, c_re, c_im, d_skip):
    hi = lax.Precision.HIGHEST
    L, G, N, P = SSM_SUB, SSM_GROUPS, SSM_STATE, SSM_GROUP
    nb, gb = SSM_NB, SSM_BUNDLE
    dt = jnp.exp(log_dt.astype(_F32))[:, None]
    lr, li = a_re.astype(_F32), a_im.astype(_F32)
    mag = jnp.exp(lr * dt)
    abar_re, abar_im = mag * jnp.cos(li * dt), mag * jnp.sin(li * dt)
    den = lr * lr + li * li
    nr, ni = abar_re - 1.0, abar_im
    fr = (nr * lr + ni * li) / den
    fi = (ni * lr - nr * li) / den
    br, bi = b_re.astype(_F32), b_im.astype(_F32)
    bbr = fr[..., None] * br - fi[..., None] * bi
    bbi = fr[..., None] * bi + fi[..., None] * br
    cr, ci = c_re.astype(_F32), c_im.astype(_F32)
    j = jnp.arange(L + 1, dtype=_F32)[:, None, None]
    pmag = jnp.exp(lr * dt * j)
    pr, pi_ = pmag * jnp.cos(li * dt * j), pmag * jnp.sin(li * dt * j)
    abr = pr[:L, :, :, None] * bbr - pi_[:L, :, :, None] * bbi
    abi = pr[:L, :, :, None] * bbi + pi_[:L, :, :, None] * bbr
    kj = (jnp.einsum('gpn,jgnq->jgpq', cr, abr, precision=hi)
          - jnp.einsum('gpn,jgnq->jgpq', ci, abi, precision=hi))
    kj = kj.at[0].add(jnp.eye(P, dtype=_F32) * d_skip.astype(_F32).reshape(G, P)[:, :, None])
    lag = jnp.arange(L)[None, :] - jnp.arange(L)[:, None]
    toe = jnp.where((lag >= 0)[:, :, None, None, None], kj[jnp.clip(lag, 0)], 0.0)
    toe = toe.reshape(L, L, nb, gb, P, P)
    t_c = jnp.transpose(toe, (2, 0, 3, 5, 1, 4)).reshape(nb, SSM_FLAT, L * P)
    ab = jnp.stack([abr[::-1], abi[::-1]], axis=0).reshape(2, L, nb, gb, N, P)
    s_c = jnp.transpose(ab, (2, 1, 3, 5, 0, 4)).reshape(nb, SSM_FLAT, 2 * N)
    er = cr[None] * pr[1:, :, None, :] - ci[None] * pi_[1:, :, None, :]
    ei = cr[None] * pi_[1:, :, None, :] + ci[None] * pr[1:, :, None, :]
    e = jnp.stack([er, -ei], axis=0).reshape(2, L, nb, gb, P, N)
    g_c = jnp.transpose(e, (2, 0, 3, 5, 1, 4)).reshape(nb, 2 * SSM_NS, L * P)

    def spread(compact, col_block, row_block):
        width = compact.shape[-1] * gb
        lane = np.arange(width)
        src = (lane // (gb * col_block)) * col_block + lane % col_block
        onehot = jnp.asarray(np.arange(compact.shape[-1])[:, None] == src[None, :], _BF16)
        full = jnp.einsum('brk,kc->brc', compact.astype(_BF16), onehot, preferred_element_type=_F32)
        row_g = (np.arange(compact.shape[1]) // row_block) % gb
        col_g = (lane // col_block) % gb
        return jnp.where(jnp.asarray(row_g[:, None] == col_g[None, :]), full, 0.0).astype(_BF16)

    w_t = spread(t_c, P, P)
    w_s = spread(s_c, N, P)
    w_g = spread(g_c, P, N)
    a_r = pr[L].reshape(nb, 1, SSM_NS)
    a_i = pi_[L].reshape(nb, 1, SSM_NS)
    return w_t, w_s, w_g, a_r, a_i


def _ssm_kernel(u_ref, wt_ref, ws_ref, wg_ref, ar_ref, ai_ref, y_ref, lhs_scr, v_scr, yp_scr, st_scr):
    bsz, sblk, _ = u_ref.shape
    nsub = sblk // SSM_SUB
    ns = SSM_NS

    @pl.when(pl.program_id(1) == 0)
    def _():
        st_scr[...] = jnp.zeros_like(st_scr)

    for b in range(bsz):
        for s in range(SSM_SUB):
            lhs_scr[s, pl.ds(b, nsub, stride=bsz), :] = u_ref[b, pl.ds(s, nsub, stride=SSM_SUB), :]
    lhs = jnp.concatenate([lhs_scr[s] for s in range(SSM_SUB)], axis=-1).astype(_BF16)
    v_scr[...] = jnp.dot(lhs, ws_ref[0], preferred_element_type=_F32)

    ar = jnp.broadcast_to(ar_ref[0], (bsz, ns))
    ai = jnp.broadcast_to(ai_ref[0], (bsz, ns))

    def step(c, carry):
        sr, si = carry
        r0 = pl.multiple_of(c * bsz, bsz)
        vr = v_scr[pl.ds(r0, bsz), 0:ns]
        vi = v_scr[pl.ds(r0, bsz), ns:2 * ns]
        v_scr[pl.ds(r0, bsz), 0:ns] = sr
        v_scr[pl.ds(r0, bsz), ns:2 * ns] = si
        return ar * sr - ai * si + vr, ar * si + ai * sr + vi

    sr, si = lax.fori_loop(0, nsub, step, (st_scr[0], st_scr[1]), unroll=4)
    st_scr[0] = sr
    st_scr[1] = si

    yp = (jnp.dot(lhs, wt_ref[0], preferred_element_type=_F32)
          + jnp.dot(v_scr[...].astype(_BF16), wg_ref[0], preferred_element_type=_F32))
    for t in range(SSM_SUB):
        yp_scr[t] = yp[:, t * LANES:(t + 1) * LANES]
    for b in range(bsz):
        for t in range(SSM_SUB):
            y_ref[b, pl.ds(t, nsub, stride=SSM_SUB), :] = yp_scr[t, pl.ds(b, nsub, stride=bsz), :]


def _ssm(u, w_t, w_s, w_g, a_r, a_i, sblk):
    bsz, seq, _ = u.shape
    rows = (sblk // SSM_SUB) * bsz
    tok = pl.BlockSpec((bsz, sblk, LANES), lambda g, i: (0, i, g))
    wspec = lambda shape: pl.BlockSpec((1,) + shape, lambda g, i: (g, 0, 0))
    return pl.pallas_call(
        _ssm_kernel,
        grid=(SSM_NB, seq // sblk),
        in_specs=[tok, wspec((SSM_FLAT, SSM_FLAT)), wspec((SSM_FLAT, 2 * SSM_NS)),
                  wspec((2 * SSM_NS, SSM_FLAT)), wspec((1, SSM_NS)), wspec((1, SSM_NS))],
        out_specs=tok,
        out_shape=jax.ShapeDtypeStruct(u.shape, _F32),
        scratch_shapes=[pltpu.VMEM((SSM_SUB, rows, LANES), _F32),
                        pltpu.VMEM((rows, 2 * SSM_NS), _F32),
                        pltpu.VMEM((SSM_SUB, rows, LANES), _F32),
                        pltpu.VMEM((2, bsz, SSM_NS), _F32)],
        compiler_params=pltpu.CompilerParams(dimension_semantics=("parallel", "arbitrary"),
                                             vmem_limit_bytes=VMEM_LIMIT),
        name="ssm",
    )(u, w_t, w_s, w_g, a_r, a_i)


def _attn_kernel(qt_ref, k_ref, vt_ref, o_ref, sa_ref, sb_ref, acc_ref, pen_ref):
    t = ATT_TILE
    nq = qt_ref.shape[1]
    nh = k_ref.shape[1]
    ones = jnp.ones((ONES_ROWS, t), _BF16)

    row = lax.broadcasted_iota(jnp.int32, (HEAD_PAD, t), 0) - (QK_NOPE + QK_ROPE)
    qry_chunk = lax.broadcasted_iota(jnp.int32, (HEAD_PAD, t), 1) // CHUNK
    pen_ref[0] = jnp.zeros((HEAD_PAD, t), _BF16)
    pen_ref[1] = jnp.where((row >= 0) & (row < t // CHUNK) & (row > qry_chunk), _NEG_BIG, 0.0).astype(_BF16)

    def scores(dst_ref, qi, kj):
        k0 = pl.multiple_of(kj * t, t)
        pen = pen_ref[(kj == qi).astype(jnp.int32)]
        for hh in range(nh):
            qt = qt_ref[0, qi, hh * HEAD_PAD:(hh + 1) * HEAD_PAD, :] + pen
            dst_ref[hh] = jnp.dot(k_ref[0, hh, pl.ds(k0, t), :], qt,
                                  preferred_element_type=_F32)

    def step(cur_ref, nxt_ref, carry):
        qi, kj, ms, accs = carry
        last = kj == qi
        kj_n = jnp.where(last, 0, kj + 1)
        qi_n = jnp.where(last, qi + 1, qi)
        scores(nxt_ref, jnp.minimum(qi_n, nq - 1), kj_n)
        ms_n, accs_n = [], []
        for hh in range(nh):
            st = cur_ref[hh]
            m_new = jnp.maximum(ms[hh], jnp.max(st, axis=0, keepdims=True))
            alpha = jnp.exp2(ms[hh] - m_new)
            p = jnp.exp2(st - m_new).astype(_BF16)
            vte = jnp.concatenate([vt_ref[0, kj, hh * V_HEAD:(hh + 1) * V_HEAD, :], ones], axis=0)
            acc = alpha * accs[hh] + jnp.dot(vte, p, preferred_element_type=_F32)
            acc_ref[qi, hh] = acc
            ms_n.append(jnp.where(last, _NEG_BIG, m_new))
            accs_n.append(jnp.where(last, 0.0, acc))
        return qi_n, kj_n, tuple(ms_n), tuple(accs_n)

    zero = jnp.int32(0)
    scores(sa_ref, zero, zero)
    init = (zero, zero,
            tuple(jnp.full((1, t), _NEG_BIG, _F32) for _ in range(nh)),
            tuple(jnp.zeros((V_HEAD + ONES_ROWS, t), _F32) for _ in range(nh)))
    nsteps = nq * (nq + 1) // 2
    carry = lax.fori_loop(0, nsteps // 2,
                          lambda _, c: step(sb_ref, sa_ref, step(sa_ref, sb_ref, c)), init)
    if nsteps % 2:
        step(sa_ref, sb_ref, carry)

    def finish(qi, _):
        outs = [acc_ref[qi, hh, :V_HEAD] / acc_ref[qi, hh, V_HEAD:V_HEAD + 1] for hh in range(nh)]
        o_ref[0, pl.ds(pl.multiple_of(qi * t, t), t), :] = (
            jnp.transpose(jnp.concatenate(outs, axis=0)).astype(_BF16))
        return 0

    lax.fori_loop(0, nq, finish, 0)


def _attn(qt, k, vt):
    bsz, heads, seq, _ = k.shape
    nh = ATT_HEADS
    k_spec = pl.BlockSpec((1, nh, seq, HEAD_PAD), lambda b, h: (b, h, 0, 0))
    tiled = lambda rows: pl.BlockSpec((1, seq // ATT_TILE, nh * rows, ATT_TILE), lambda b, h: (b, 0, h, 0))
    return pl.pallas_call(
        _attn_kernel,
        grid=(bsz, heads // nh),
        in_specs=[tiled(HEAD_PAD), k_spec, tiled(V_HEAD)],
        out_specs=pl.BlockSpec((1, seq, nh * V_HEAD), lambda b, h: (b, 0, h)),
        out_shape=jax.ShapeDtypeStruct((bsz, seq, MLA_WIDTH), _BF16),
        scratch_shapes=[pltpu.VMEM((nh, ATT_TILE, ATT_TILE), _F32),
                        pltpu.VMEM((nh, ATT_TILE, ATT_TILE), _F32),
                        pltpu.VMEM((seq // ATT_TILE, nh, V_HEAD + ONES_ROWS, ATT_TILE), _F32),
                        pltpu.VMEM((2, HEAD_PAD, ATT_TILE), _BF16)],
        compiler_params=pltpu.CompilerParams(dimension_semantics=("parallel", "parallel"),
                                             vmem_limit_bytes=VMEM_LIMIT),
        name="attn",
    )(qt, k, vt)


def _post_kernel(x_ref, gate_ref, y_ref, szs_ref, sgs_ref, o_ref, szm_ref, sgm_ref,
                 wglu_ref, bglu_ref, wbs_ref, wbm_ref, wout_ref, gpost_ref, out_ref):
    ge = _gelu_tanh(y_ref[0]).astype(_BF16)
    gl = jnp.dot(ge, wglu_ref[...], preferred_element_type=_F32) + bglu_ref[...]
    a = (gl[:, :SSM_WIDTH] * _sigmoid(gl[:, SSM_WIDTH:])) * szs_ref[0].astype(_F32)
    ys = jnp.dot(a.astype(_BF16), wbs_ref[...], preferred_element_type=_F32)
    am = o_ref[0].astype(_F32) * szm_ref[0].astype(_F32)
    ym = jnp.dot(am.astype(_BF16), wbm_ref[...], preferred_element_type=_F32)
    merged = sgs_ref[0].astype(_F32) * ys + sgm_ref[0].astype(_F32) * ym
    out = jnp.dot(merged.astype(_BF16), wout_ref[...], preferred_element_type=_F32)
    out_ref[0] = x_ref[0] + gate_ref[0] * (_rms(out) * gpost_ref[...])


def _post(x, gate, y, szs, sgs, o, szm, sgm, wglu, bglu, wbs, wbm, wout, gpost, tm):
    bsz, seq, _ = x.shape
    tok = pl.BlockSpec((1, tm, D_MODEL), lambda b, i: (b, i, 0))
    per_b = pl.BlockSpec((1, 1, D_MODEL), lambda b, i: (b, 0, 0))
    consts = [wglu, bglu, wbs, wbm, wout, gpost]
    return pl.pallas_call(
        _post_kernel,
        grid=(bsz, seq // tm),
        in_specs=[tok, per_b] + [tok] * 6 + [_const_spec(w.shape) for w in consts],
        out_specs=tok,
        out_shape=jax.ShapeDtypeStruct(x.shape, x.dtype),
        compiler_params=pltpu.CompilerParams(dimension_semantics=("parallel", "parallel"),
                                             vmem_limit_bytes=VMEM_LIMIT),
        name="post",
    )(x, gate, y, szs, sgs, o, szm, sgm, *consts)


def _mla_weights(w_q_up, w_kv_up):
    H = MLA_HEADS
    wq = w_q_up.reshape(Q_LORA, H, QK_NOPE + QK_ROPE)
    zq = jnp.zeros((Q_LORA, H, HEAD_PAD - QK_NOPE - QK_ROPE), w_q_up.dtype)
    wq_t = jnp.transpose(jnp.concatenate([wq, zq], axis=-1).reshape(Q_LORA, H * HEAD_PAD))
    wkv = w_kv_up.reshape(KV_LORA, H, QK_NOPE + V_HEAD)
    zk = jnp.zeros((KV_LORA, H, HEAD_PAD - QK_NOPE), w_kv_up.dtype)
    wk_main = jnp.concatenate([wkv[..., :QK_NOPE], zk], axis=-1).reshape(KV_LORA, H * HEAD_PAD)
    wv_t = jnp.transpose(wkv[..., QK_NOPE:].reshape(KV_LORA, H * V_HEAD))
    return [w.astype(_BF16) for w in (wq_t, wk_main, wv_t)]


def _rope_inv_freq_col():
    inv_freq = ROPE_BASE ** (-jnp.arange(0, QK_ROPE, 2, dtype=_F32) / QK_ROPE)
    return inv_freq.reshape(QK_ROPE // 2, 1)


def kernel(x, c, positions, w_ada, b_ada, g_pre, w_in, ssm_log_dt, ssm_a_re, ssm_a_im, ssm_b_re, ssm_b_im, ssm_c_re, ssm_c_im, ssm_d, w_glu, b_glu, g_q_norm, w_q_up, g_kv_norm, w_kv_up, w_br_ssm, w_br_mla, w_out, g_post):
    bsz, seq, _ = x.shape
    depth = w_ada.shape[0]
    assert bsz == SUBLANES and seq % ATT_TILE == 0
    tm = ATT_TILE
    sblk = min(512, seq)
    pos3 = positions.reshape(bsz, 1, seq)
    invf = _rope_inv_freq_col()
    off = np.cumsum((0, SSM_WIDTH, SSM_WIDTH, Q_LORA, KV_LORA, QK_ROPE, MLA_WIDTH, D_MODEL, D_MODEL))

    for l in range(depth):
        mod = _ada(c, w_ada[l], b_ada[l])
        shift, scale, gate = (mod[:, None, i * D_MODEL:(i + 1) * D_MODEL] for i in range(3))

        wb = w_in[l].astype(_BF16)
        seg = lambda i: wb[:, off[i]:off[i + 1]]
        w_lat = jnp.concatenate([seg(2), seg(3)], axis=1)
        w_krt = jnp.concatenate(
            [jnp.transpose(seg(4)), jnp.zeros((HEAD_PAD - QK_ROPE, D_MODEL), _BF16)], axis=0)
        ws = [seg(0), seg(1), seg(5), seg(6), seg(7), w_lat, w_krt,
              g_q_norm[l].reshape(1, Q_LORA), g_kv_norm[l].reshape(1, KV_LORA)]
        ws += _mla_weights(w_q_up[l], w_kv_up[l])
        u, szs, szm, sgs, sgm, qt, k, vt = _inproj(
            x, shift, scale, g_pre[l].reshape(1, D_MODEL), pos3, invf, ws, tm)

        w_t, w_s, w_g, a_r, a_i = _ssm_weights(
            ssm_log_dt[l], ssm_a_re[l], ssm_a_im[l], ssm_b_re[l], ssm_b_im[l],
            ssm_c_re[l], ssm_c_im[l], ssm_d[l])
        y = _ssm(u, w_t, w_s, w_g, a_r, a_i, sblk)

        o = _attn(qt, k, vt)

        x = _post(x, gate, y, szs, sgs, o, szm, sgm,
                  w_glu[l].astype(_BF16), b_glu[l].reshape(1, -1), w_br_ssm[l].astype(_BF16),
                  w_br_mla[l].astype(_BF16), w_out[l].astype(_BF16), g_post[l].reshape(1, D_MODEL), tm)
    return x
```

```python
import functools
import math

import jax
import jax.numpy as jnp
import numpy as np
from jax import lax
from jax.experimental import pallas as pl
from jax.experimental.pallas import tpu as pltpu

D_MODEL = 1024
CHUNK = 64
SSM_WIDTH = 1024
SSM_GROUP = 16
SSM_GROUPS = SSM_WIDTH // SSM_GROUP
SSM_STATE = 64
MLA_HEADS = 16
QK_NOPE = 64
QK_ROPE = 32
V_HEAD = 64
Q_LORA = 256
KV_LORA = 256
MLA_WIDTH = MLA_HEADS * V_HEAD
ROPE_BASE = 10000.0
EPS = 1e-6

LANES = 128
SUBLANES = 8
HEAD_PAD = 128
SSM_SUB = 8
SSM_BUNDLE = LANES // SSM_GROUP
SSM_NB = SSM_GROUPS // SSM_BUNDLE
SSM_FLAT = SSM_SUB * LANES
SSM_NS = SSM_BUNDLE * SSM_STATE
ATT_TILE = 256
ATT_HEADS = 4
ATT_UNROLL = 8
POST_TILE = 512
ONES_ROWS = 16
VMEM_LIMIT = 56 * 1024 * 1024

_F32 = jnp.float32
_BF16 = jnp.bfloat16
_SOFTMAX_SCALE_LOG2E = ((QK_NOPE + QK_ROPE) ** -0.5) * math.log2(math.e)
_NEG_BIG = -1e30


def _const_spec(shape):
    zeros = (0,) * len(shape)
    return pl.BlockSpec(shape, lambda *_: zeros, pipeline_mode=pl.Buffered(1))


def _sigmoid(x):
    return 1.0 / (1.0 + jnp.exp(-x))


def _gelu_tanh(x):
    return 0.5 * x * (1.0 + jnp.tanh(math.sqrt(2.0 / math.pi) * (x + 0.044715 * (x * x * x))))


def _rms(x):
    return x * lax.rsqrt(jnp.mean(x * x, axis=-1, keepdims=True) + EPS)


def _ada_kernel(c_ref, w_ref, b_ref, o_ref):
    o_ref[...] = jnp.dot(c_ref[...], w_ref[...], preferred_element_type=_F32) + b_ref[...]


def _ada(c, w, b):
    bsz = c.shape[0]
    n = w.shape[1]
    bn = D_MODEL
    return pl.pallas_call(
        _ada_kernel,
        grid=(n // bn,),
        in_specs=[pl.BlockSpec((bsz, D_MODEL), lambda j: (0, 0)),
                  pl.BlockSpec((D_MODEL, bn), lambda j: (0, j)),
                  pl.BlockSpec((1, bn), lambda j: (0, j))],
        out_specs=pl.BlockSpec((bsz, bn), lambda j: (0, j)),
        out_shape=jax.ShapeDtypeStruct((bsz, n), _F32),
        compiler_params=pltpu.CompilerParams(dimension_semantics=("arbitrary",),
                                             vmem_limit_bytes=VMEM_LIMIT),
        name="ada",
    )(c, w, b.reshape(1, n))


def _inproj_kernel(x_ref, shift_ref, scale_ref, gpre_ref, pos_ref, invf_ref,
                   w_u_ref, w_zs_ref, w_zm_ref, w_gs_ref, w_gm_ref, w_lat_ref, w_krt_ref,
                   gq_ref, gkv_ref, wqt_ref, wk_ref, wvt_ref,
                   u_ref, szs_ref, szm_ref, sgs_ref, sgm_ref, qt_ref, k_ref, vt_ref):
    tm = x_ref.shape[1]
    half = QK_ROPE // 2
    r1 = slice(QK_NOPE, QK_NOPE + half)
    r2 = slice(QK_NOPE + half, QK_NOPE + QK_ROPE)
    nt = (((1,), (1,)), ((), ()))
    x = x_ref[0]
    h = _rms(x) * gpre_ref[...] * (1.0 + scale_ref[0]) + shift_ref[0]
    hb = h.astype(_BF16)

    def proj(w_ref):
        return jnp.dot(hb, w_ref[...], preferred_element_type=_F32)

    u_ref[0] = proj(w_u_ref)
    z = proj(w_zs_ref)
    szs_ref[0] = (z * _sigmoid(z)).astype(_BF16)
    z = proj(w_zm_ref)
    szm_ref[0] = (z * _sigmoid(z)).astype(_BF16)
    sgs_ref[0] = _sigmoid(proj(w_gs_ref)).astype(_BF16)
    sgm_ref[0] = _sigmoid(proj(w_gm_ref)).astype(_BF16)

    lat = proj(w_lat_ref)
    qn = (_rms(lat[:, :Q_LORA]) * gq_ref[...]).astype(_BF16)
    kvn = (_rms(lat[:, Q_LORA:]) * gkv_ref[...]).astype(_BF16)

    ang = invf_ref[...] * pos_ref[0].astype(_F32)
    cos = jnp.cos(ang)
    sin = jnp.sin(ang)

    qt = lax.dot_general(wqt_ref[...], qn, nt, preferred_element_type=_F32)
    for hd in range(MLA_HEADS):
        blk = qt[hd * HEAD_PAD:(hd + 1) * HEAD_PAD]
        t1, t2 = blk[r1], blk[r2]
        blk = jnp.concatenate([blk[:QK_NOPE], t1 * cos - t2 * sin, t2 * cos + t1 * sin,
                               blk[QK_NOPE + QK_ROPE:]], axis=0)
        qt_ref[0, 0, hd * HEAD_PAD:(hd + 1) * HEAD_PAD, :] = (blk * _SOFTMAX_SCALE_LOG2E).astype(_BF16)

    krt = lax.dot_general(w_krt_ref[...], hb, nt, preferred_element_type=_F32)
    t1, t2 = krt[:half], krt[half:QK_ROPE]
    spare = HEAD_PAD - QK_NOPE - QK_ROPE
    row = lax.broadcasted_iota(jnp.int32, (spare, tm), 0)
    col_chunk = lax.broadcasted_iota(jnp.int32, (spare, tm), 1) // CHUNK
    onehot = jnp.where(row == col_chunk, 1.0, 0.0)
    kr_full = jnp.concatenate([jnp.zeros((QK_NOPE, tm), _F32), t1 * cos - t2 * sin,
                               t2 * cos + t1 * sin, onehot], axis=0)
    kr_place = jnp.transpose(kr_full)
    ka = jnp.dot(kvn, wk_ref[...], preferred_element_type=_F32)
    for hd in range(MLA_HEADS):
        k_ref[0, hd] = (ka[:, hd * HEAD_PAD:(hd + 1) * HEAD_PAD] + kr_place).astype(_BF16)
    vt_ref[0, 0] = lax.dot_general(wvt_ref[...], kvn, nt, preferred_element_type=_F32).astype(_BF16)


def _inproj(x, shift, scale, g_pre, pos3, invf, ws, tm):
    bsz, seq, _ = x.shape
    tok = lambda w: pl.BlockSpec((1, tm, w), lambda b, i: (b, i, 0))
    per_b = pl.BlockSpec((1, 1, D_MODEL), lambda b, i: (b, 0, 0))
    head = pl.BlockSpec((1, MLA_HEADS, tm, HEAD_PAD), lambda b, i: (b, 0, i, 0))
    tiled = lambda rows: pl.BlockSpec((1, 1, rows, tm), lambda b, i: (b, i, 0, 0))
    tshape = lambda rows: jax.ShapeDtypeStruct((bsz, seq // tm, rows, tm), _BF16)
    act = jax.ShapeDtypeStruct((bsz, seq, D_MODEL), _BF16)
    w_specs = [_const_spec(w.shape) for w in ws]
    return pl.pallas_call(
        _inproj_kernel,
        grid=(bsz, seq // tm),
        in_specs=[tok(D_MODEL), per_b, per_b, _const_spec((1, D_MODEL)),
                  pl.BlockSpec((1, 1, tm), lambda b, i: (b, 0, i)),
                  _const_spec((QK_ROPE // 2, 1))] + w_specs,
        out_specs=[tok(D_MODEL)] * 5 + [tiled(MLA_HEADS * HEAD_PAD), head, tiled(MLA_WIDTH)],
        out_shape=[jax.ShapeDtypeStruct((bsz, seq, SSM_WIDTH), _F32)] + [act] * 4
                  + [tshape(MLA_HEADS * HEAD_PAD),
                     jax.ShapeDtypeStruct((bsz, MLA_HEADS, seq, HEAD_PAD), _BF16), tshape(MLA_WIDTH)],
        compiler_params=pltpu.CompilerParams(dimension_semantics=("parallel", "parallel"),
                                             vmem_limit_bytes=VMEM_LIMIT),
        name="inproj",
    )(x, shift, scale, g_pre, pos3, invf, *ws)


def _ssm_weights(log_dt, a_re, a_im, b_re, b_im, c_re, c_im, d_skip):
    hi = lax.Precision.HIGHEST
    L, G, N, P = SSM_SUB, SSM_GROUPS, SSM_STATE, SSM_GROUP
    nb, gb = SSM_NB, SSM_BUNDLE
    dt = jnp.exp(log_dt.astype(_F32))[:, None]
    lr, li = a_re.astype(_F32), a_im.astype(_F32)
    mag = jnp.exp(lr * dt)
    abar_re, abar_im = mag * jnp.cos(li * dt), mag * jnp.sin(li * dt)
    den = lr * lr + li * li
    nr, ni = abar_re - 1.0, abar_im
    fr = (nr * lr + ni * li) / den
    fi = (ni * lr - nr * li) / den
    br, bi = b_re.astype(_F32), b_im.astype(_F32)
    bbr = fr[..., None] * br - fi[..., None] * bi
    bbi = fr[..., None] * bi + fi[..., None] * br
    cr, ci = c_re.astype(_F32), c_im.astype(_F32)
    j = jnp.arange(L + 1, dtype=_F32)[:, None, None]
    pmag = jnp.exp(lr * dt * j)
    pr, pi_ = pmag * jnp.cos(li * dt * j), pmag * jnp.sin(li * dt * j)
    abr = pr[:L, :, :, None] * bbr - pi_[:L, :, :, None] * bbi
    abi = pr[:L, :, :, None] * bbi + pi_[:L, :, :, None] * bbr
    kj = (jnp.einsum('gpn,jgnq->jgpq', cr, abr, precision=hi)
          - jnp.einsum('gpn,jgnq->jgpq', ci, abi, precision=hi))
    kj = kj.at[0].add(jnp.eye(P, dtype=_F32) * d_skip.astype(_F32).reshape(G, P)[:, :, None])
    lag = jnp.arange(L)[None, :] - jnp.arange(L)[:, None]
    toe = jnp.where((lag >= 0)[:, :, None, None, None], kj[jnp.clip(lag, 0)], 0.0)
    toe = toe.reshape(L, L, nb, gb, P, P)
    t_c = jnp.transpose(toe, (2, 0, 3, 5, 1, 4)).reshape(nb, SSM_FLAT, L * P)
    ab = jnp.stack([abr[::-1], abi[::-1]], axis=0).reshape(2, L, nb, gb, N, P)
    s_c = jnp.transpose(ab, (2, 1, 3, 5, 0, 4)).reshape(nb, SSM_FLAT, 2 * N)
    er = cr[None] * pr[1:, :, None, :] - ci[None] * pi_[1:, :, None, :]
    ei = cr[None] * pi_[1:, :, None, :] + ci[None] * pr[1:, :, None, :]
    e = jnp.stack([er, -ei], axis=0).reshape(2, L, nb, gb, P, N)
    g_c = jnp.transpose(e, (2, 0, 3, 5, 1, 4)).reshape(nb, 2 * SSM_NS, L * P)

    def spread(compact, col_block, row_block):
        width = compact.shape[-1] * gb
        lane = np.arange(width)
        src = (lane // (gb * col_block)) * col_block + lane % col_block
        onehot = jnp.asarray(np.arange(compact.shape[-1])[:, None] == src[None, :], _BF16)
        full = jnp.einsum('brk,kc->brc', compact.astype(_BF16), onehot, preferred_element_type=_F32)
        row_g = (np.arange(compact.shape[1]) // row_block) % gb
        col_g = (lane // col_block) % gb
        return jnp.where(jnp.asarray(row_g[:, None] == col_g[None, :]), full, 0.0).astype(_BF16)

    w_t = spread(t_c, P, P)
    w_s = spread(s_c, N, P)
    w_g = spread(g_c, P, N)
    a_r = pr[L].reshape(nb, 1, SSM_NS)
    a_i = pi_[L].reshape(nb, 1, SSM_NS)
    return w_t, w_s, w_g, a_r, a_i


def _ssm_kernel(u_ref, wt_ref, ws_ref, wg_ref, ar_ref, ai_ref, y_ref, lhs_scr, v_scr, yp_scr, st_scr):
    bsz, sblk, _ = u_ref.shape
    nsub = sblk // SSM_SUB
    ns = SSM_NS

    @pl.when(pl.program_id(1) == 0)
    def _():
        st_scr[...] = jnp.zeros_like(st_scr)

    for b in range(bsz):
        for s in range(SSM_SUB):
            lhs_scr[s, pl.ds(b, nsub, stride=bsz), :] = u_ref[b, pl.ds(s, nsub, stride=SSM_SUB), :]
    lhs = jnp.concatenate([lhs_scr[s] for s in range(SSM_SUB)], axis=-1).astype(_BF16)
    v_scr[...] = jnp.dot(lhs, ws_ref[0], preferred_element_type=_F32)

    ar = jnp.broadcast_to(ar_ref[0], (bsz, ns))
    ai = jnp.broadcast_to(ai_ref[0], (bsz, ns))

    def step(c, carry):
        sr, si = carry
        r0 = pl.multiple_of(c * bsz, bsz)
        vr = v_scr[pl.ds(r0, bsz), 0:ns]
        vi = v_scr[pl.ds(r0, bsz), ns:2 * ns]
        v_scr[pl.ds(r0, bsz), 0:ns] = sr
        v_scr[pl.ds(r0, bsz), ns:2 * ns] = si
        return ar * sr - ai * si + vr, ar * si + ai * sr + vi

    sr, si = lax.fori_loop(0, nsub, step, (st_scr[0], st_scr[1]), unroll=4)
    st_scr[0] = sr
    st_scr[1] = si

    carry_in = jnp.dot(v_scr[...].astype(_BF16), wg_ref[0], preferred_element_type=_F32)
    pair = 2 * LANES
    for j in range(SSM_SUB // 2):
        hi = (j + 1) * pair
        yp = carry_in[:, j * pair:hi] + jnp.dot(lhs[:, :hi], wt_ref[0, :hi, j * pair:hi],
                                                 preferred_element_type=_F32)
        yp_scr[2 * j] = yp[:, :LANES]
        yp_scr[2 * j + 1] = yp[:, LANES:]
    for b in range(bsz):
        for t in range(SSM_SUB):
            y_ref[b, pl.ds(t, nsub, stride=SSM_SUB), :] = yp_scr[t, pl.ds(b, nsub, stride=bsz), :]


def _ssm(u, w_t, w_s, w_g, a_r, a_i, sblk):
    bsz, seq, _ = u.shape
    rows = (sblk // SSM_SUB) * bsz
    tok = pl.BlockSpec((bsz, sblk, LANES), lambda g, i: (0, i, g))
    wspec = lambda shape: pl.BlockSpec((1,) + shape, lambda g, i: (g, 0, 0))
    return pl.pallas_call(
        _ssm_kernel,
        grid=(SSM_NB, seq // sblk),
        in_specs=[tok, wspec((SSM_FLAT, SSM_FLAT)), wspec((SSM_FLAT, 2 * SSM_NS)),
                  wspec((2 * SSM_NS, SSM_FLAT)), wspec((1, SSM_NS)), wspec((1, SSM_NS))],
        out_specs=tok,
        out_shape=jax.ShapeDtypeStruct(u.shape, _F32),
        scratch_shapes=[pltpu.VMEM((SSM_SUB, rows, LANES), _F32),
                        pltpu.VMEM((rows, 2 * SSM_NS), _F32),
                        pltpu.VMEM((SSM_SUB, rows, LANES), _F32),
                        pltpu.VMEM((2, bsz, SSM_NS), _F32)],
        compiler_params=pltpu.CompilerParams(dimension_semantics=("parallel", "arbitrary"),
                                             vmem_limit_bytes=VMEM_LIMIT),
        name="ssm",
    )(u, w_t, w_s, w_g, a_r, a_i)


def _attn_kernel(qt_ref, k_ref, vt_ref, o_ref, sa_ref, sb_ref, acc_ref, pen_ref):
    t = ATT_TILE
    nq = qt_ref.shape[1]
    nh = k_ref.shape[1]
    ones = jnp.ones((ONES_ROWS, t), _BF16)

    row = lax.broadcasted_iota(jnp.int32, (HEAD_PAD, t), 0) - (QK_NOPE + QK_ROPE)
    qry_chunk = lax.broadcasted_iota(jnp.int32, (HEAD_PAD, t), 1) // CHUNK
    pen_ref[0] = jnp.zeros((HEAD_PAD, t), _BF16)
    pen_ref[1] = jnp.where((row >= 0) & (row < t // CHUNK) & (row > qry_chunk), _NEG_BIG, 0.0).astype(_BF16)

    def scores(dst_ref, qi, kj):
        k0 = pl.multiple_of(kj * t, t)
        pen = pen_ref[(kj == qi).astype(jnp.int32)]
        for hh in range(nh):
            qt = qt_ref[0, qi, hh * HEAD_PAD:(hh + 1) * HEAD_PAD, :] + pen
            dst_ref[hh] = jnp.dot(k_ref[0, hh, pl.ds(k0, t), :], qt,
                                  preferred_element_type=_F32)

    def step(cur_ref, nxt_ref, carry):
        qi, kj, ms, accs = carry
        last = kj == qi
        kj_n = jnp.where(last, 0, kj + 1)
        qi_n = jnp.where(last, qi + 1, qi)
        scores(nxt_ref, jnp.minimum(qi_n, nq - 1), kj_n)
        ms_n, accs_n = [], []
        for hh in range(nh):
            st = cur_ref[hh]
            m_new = jnp.maximum(ms[hh], jnp.max(st, axis=0, keepdims=True))
            alpha = jnp.exp2(ms[hh] - m_new)
            p = jnp.exp2(st - m_new).astype(_BF16)
            vte = jnp.concatenate([vt_ref[0, kj, hh * V_HEAD:(hh + 1) * V_HEAD, :], ones], axis=0)
            acc = alpha * accs[hh] + jnp.dot(vte, p, preferred_element_type=_F32)
            acc_ref[qi, hh] = acc
            ms_n.append(jnp.where(last, _NEG_BIG, m_new))
            accs_n.append(jnp.where(last, 0.0, acc))
        return qi_n, kj_n, tuple(ms_n), tuple(accs_n)

    zero = jnp.int32(0)
    scores(sa_ref, zero, zero)
    init = (zero, zero,
            tuple(jnp.full((1, t), _NEG_BIG, _F32) for _ in range(nh)),
            tuple(jnp.zeros((V_HEAD + ONES_ROWS, t), _F32) for _ in range(nh)))
    nsteps = nq * (nq + 1) // 2
    bufs = (sa_ref, sb_ref)

    def run(count, c):
        for i in range(count):
            c = step(bufs[i % 2], bufs[(i + 1) % 2], c)
        return c

    assert ATT_UNROLL % 2 == 0
    carry = lax.fori_loop(0, nsteps // ATT_UNROLL, lambda _, c: run(ATT_UNROLL, c), init)
    run(nsteps % ATT_UNROLL, carry)

    def finish(qi, _):
        outs = [acc_ref[qi, hh, :V_HEAD] / acc_ref[qi, hh, V_HEAD:V_HEAD + 1] for hh in range(nh)]
        o_ref[0, pl.ds(pl.multiple_of(qi * t, t), t), :] = (
            jnp.transpose(jnp.concatenate(outs, axis=0)).astype(_BF16))
        return 0

    lax.fori_loop(0, nq, finish, 0)


def _attn(qt, k, vt):
    bsz, heads, seq, _ = k.shape
    nh = ATT_HEADS
    k_spec = pl.BlockSpec((1, nh, seq, HEAD_PAD), lambda b, h: (b, h, 0, 0))
    tiled = lambda rows: pl.BlockSpec((1, seq // ATT_TILE, nh * rows, ATT_TILE), lambda b, h: (b, 0, h, 0))
    return pl.pallas_call(
        _attn_kernel,
        grid=(bsz, heads // nh),
        in_specs=[tiled(HEAD_PAD), k_spec, tiled(V_HEAD)],
        out_specs=pl.BlockSpec((1, seq, nh * V_HEAD), lambda b, h: (b, 0, h)),
        out_shape=jax.ShapeDtypeStruct((bsz, seq, MLA_WIDTH), _BF16),
        scratch_shapes=[pltpu.VMEM((nh, ATT_TILE, ATT_TILE), _F32),
                        pltpu.VMEM((nh, ATT_TILE, ATT_TILE), _F32),
                        pltpu.VMEM((seq // ATT_TILE, nh, V_HEAD + ONES_ROWS, ATT_TILE), _F32),
                        pltpu.VMEM((2, HEAD_PAD, ATT_TILE), _BF16)],
        compiler_params=pltpu.CompilerParams(dimension_semantics=("parallel", "parallel"),
                                             vmem_limit_bytes=VMEM_LIMIT),
        name="attn",
    )(qt, k, vt)


def _post_kernel(x_ref, gate_ref, y_ref, szs_ref, sgs_ref, o_ref, szm_ref, sgm_ref,
                 wglu_ref, bglu_ref, wbs_ref, wbm_ref, wout_ref, gpost_ref, out_ref):
    ge = _gelu_tanh(y_ref[0]).astype(_BF16)
    gl = jnp.dot(ge, wglu_ref[...], preferred_element_type=_F32) + bglu_ref[...]
    a = (gl[:, :SSM_WIDTH] * _sigmoid(gl[:, SSM_WIDTH:])) * szs_ref[0].astype(_F32)
    ys = jnp.dot(a.astype(_BF16), wbs_ref[...], preferred_element_type=_F32)
    am = o_ref[0].astype(_F32) * szm_ref[0].astype(_F32)
    ym = jnp.dot(am.astype(_BF16), wbm_ref[...], preferred_element_type=_F32)
    merged = sgs_ref[0].astype(_F32) * ys + sgm_ref[0].astype(_F32) * ym
    out = jnp.dot(merged.astype(_BF16), wout_ref[...], preferred_element_type=_F32)
    out_ref[0] = x_ref[0] + gate_ref[0] * (_rms(out) * gpost_ref[...])


def _post(x, gate, y, szs, sgs, o, szm, sgm, wglu, bglu, wbs, wbm, wout, gpost, tm):
    bsz, seq, _ = x.shape
    tok = pl.BlockSpec((1, tm, D_MODEL), lambda b, i: (b, i, 0))
    per_b = pl.BlockSpec((1, 1, D_MODEL), lambda b, i: (b, 0, 0))
    consts = [wglu, bglu, wbs, wbm, wout, gpost]
    return pl.pallas_call(
        _post_kernel,
        grid=(bsz, seq // tm),
        in_specs=[tok, per_b] + [tok] * 6 + [_const_spec(w.shape) for w in consts],
        out_specs=tok,
        out_shape=jax.ShapeDtypeStruct(x.shape, x.dtype),
        compiler_params=pltpu.CompilerParams(dimension_semantics=("parallel", "parallel"),
                                             vmem_limit_bytes=VMEM_LIMIT),
        name="post",
    )(x, gate, y, szs, sgs, o, szm, sgm, *consts)


def _mla_weights(w_q_up, w_kv_up):
    H = MLA_HEADS
    wq = w_q_up.reshape(Q_LORA, H, QK_NOPE + QK_ROPE)
    zq = jnp.zeros((Q_LORA, H, HEAD_PAD - QK_NOPE - QK_ROPE), w_q_up.dtype)
    wq_t = jnp.transpose(jnp.concatenate([wq, zq], axis=-1).reshape(Q_LORA, H * HEAD_PAD))
    wkv = w_kv_up.reshape(KV_LORA, H, QK_NOPE + V_HEAD)
    zk = jnp.zeros((KV_LORA, H, HEAD_PAD - QK_NOPE), w_kv_up.dtype)
    wk_main = jnp.concatenate([wkv[..., :QK_NOPE], zk], axis=-1).reshape(KV_LORA, H * HEAD_PAD)
    wv_t = jnp.transpose(wkv[..., QK_NOPE:].reshape(KV_LORA, H * V_HEAD))
    return [w.astype(_BF16) for w in (wq_t, wk_main, wv_t)]


def _rope_inv_freq_col():
    inv_freq = ROPE_BASE ** (-jnp.arange(0, QK_ROPE, 2, dtype=_F32) / QK_ROPE)
    return inv_freq.reshape(QK_ROPE // 2, 1)


def kernel(x, c, positions, w_ada, b_ada, g_pre, w_in, ssm_log_dt, ssm_a_re, ssm_a_im, ssm_b_re, ssm_b_im, ssm_c_re, ssm_c_im, ssm_d, w_glu, b_glu, g_q_norm, w_q_up, g_kv_norm, w_kv_up, w_br_ssm, w_br_mla, w_out, g_post):
    bsz, seq, _ = x.shape
    depth = w_ada.shape[0]
    assert bsz == SUBLANES and seq % ATT_TILE == 0
    tm = ATT_TILE
    sblk = min(512, seq)
    pos3 = positions.reshape(bsz, 1, seq)
    invf = _rope_inv_freq_col()
    off = np.cumsum((0, SSM_WIDTH, SSM_WIDTH, Q_LORA, KV_LORA, QK_ROPE, MLA_WIDTH, D_MODEL, D_MODEL))

    for l in range(depth):
        mod = _ada(c, w_ada[l], b_ada[l])
        shift, scale, gate = (mod[:, None, i * D_MODEL:(i + 1) * D_MODEL] for i in range(3))

        wb = w_in[l].astype(_BF16)
        seg = lambda i: wb[:, off[i]:off[i + 1]]
        w_lat = jnp.concatenate([seg(2), seg(3)], axis=1)
        w_krt = jnp.concatenate(
            [jnp.transpose(seg(4)), jnp.zeros((HEAD_PAD - QK_ROPE, D_MODEL), _BF16)], axis=0)
        ws = [seg(0), seg(1), seg(5), seg(6), seg(7), w_lat, w_krt,
              g_q_norm[l].reshape(1, Q_LORA), g_kv_norm[l].reshape(1, KV_LORA)]
        ws += _mla_weights(w_q_up[l], w_kv_up[l])
        u, szs, szm, sgs, sgm, qt, k, vt = _inproj(
            x, shift, scale, g_pre[l].reshape(1, D_MODEL), pos3, invf, ws, tm)

        w_t, w_s, w_g, a_r, a_i = _ssm_weights(
            ssm_log_dt[l], ssm_a_re[l], ssm_a_im[l], ssm_b_re[l], ssm_b_im[l],
            ssm_c_re[l], ssm_c_im[l], ssm_d[l])
        y = _ssm(u, w_t, w_s, w_g, a_r, a_i, sblk)

        o = _attn(qt, k, vt)

        x = _post(x, gate, y, szs, sgs, o, szm, sgm,
                  w_glu[l].astype(_BF16), b_glu[l].reshape(1, -1), w_br_ssm[l].astype(_BF16),
                  w_br_mla[l].astype(_BF16), w_out[l].astype(_BF16), g_post[l].reshape(1, D_MODEL),
                  min(POST_TILE, seq))
    return x
```

```python
import functools
import math

import jax
import jax.numpy as jnp
import numpy as np
from jax import lax
from jax.experimental import pallas as pl
from jax.experimental.pallas import tpu as pltpu

D_MODEL = 1024
CHUNK = 64
SSM_WIDTH = 1024
SSM_GROUP = 16
SSM_GROUPS = SSM_WIDTH // SSM_GROUP
SSM_STATE = 64
MLA_HEADS = 16
QK_NOPE = 64
QK_ROPE = 32
V_HEAD = 64
Q_LORA = 256
KV_LORA = 256
MLA_WIDTH = MLA_HEADS * V_HEAD
ROPE_BASE = 10000.0
EPS = 1e-6

LANES = 128
SUBLANES = 8
HEAD_PAD = 128
SSM_SUB = 8
SSM_BUNDLE = LANES // SSM_GROUP
SSM_NB = SSM_GROUPS // SSM_BUNDLE
SSM_HALF = SSM_BUNDLE // 2
SSM_NS = SSM_BUNDLE * SSM_STATE
ATT_TILE = 256
ATT_HEADS = 4
ATT_UNROLL = 8
ATT_BUFS = 2
POST_TILE = 512
ONES_ROWS = 16
VMEM_LIMIT = 56 * 1024 * 1024

_F32 = jnp.float32
_BF16 = jnp.bfloat16
_SOFTMAX_SCALE_LOG2E = ((QK_NOPE + QK_ROPE) ** -0.5) * math.log2(math.e)
_NEG_BIG = -1e30


def _const_spec(shape):
    zeros = (0,) * len(shape)
    return pl.BlockSpec(shape, lambda *_: zeros, pipeline_mode=pl.Buffered(1))


def _sigmoid(x):
    return 1.0 / (1.0 + jnp.exp(-x))


def _gelu_tanh(x):
    return 0.5 * x * (1.0 + jnp.tanh(math.sqrt(2.0 / math.pi) * (x + 0.044715 * (x * x * x))))


def _rms(x):
    return x * lax.rsqrt(jnp.mean(x * x, axis=-1, keepdims=True) + EPS)


def _ada_kernel(c_ref, w_ref, b_ref, o_ref):
    o_ref[...] = jnp.dot(c_ref[...], w_ref[...], preferred_element_type=_F32) + b_ref[...]


def _ada(c, w, b):
    bsz = c.shape[0]
    n = w.shape[1]
    bn = D_MODEL
    return pl.pallas_call(
        _ada_kernel,
        grid=(n // bn,),
        in_specs=[pl.BlockSpec((bsz, D_MODEL), lambda j: (0, 0)),
                  pl.BlockSpec((D_MODEL, bn), lambda j: (0, j)),
                  pl.BlockSpec((1, bn), lambda j: (0, j))],
        out_specs=pl.BlockSpec((bsz, bn), lambda j: (0, j)),
        out_shape=jax.ShapeDtypeStruct((bsz, n), _F32),
        compiler_params=pltpu.CompilerParams(dimension_semantics=("arbitrary",),
                                             vmem_limit_bytes=VMEM_LIMIT),
        name="ada",
    )(c, w, b.reshape(1, n))


def _inproj_kernel(x_ref, shift_ref, scale_ref, gpre_ref, pos_ref, invf_ref,
                   w_u_ref, w_zs_ref, w_zm_ref, w_gs_ref, w_gm_ref, w_lat_ref, w_krt_ref,
                   gq_ref, gkv_ref, wqt_ref, wk_ref, wvt_ref,
                   u_ref, szs_ref, szm_ref, sgs_ref, sgm_ref, qt_ref, k_ref, vt_ref):
    tm = x_ref.shape[1]
    half = QK_ROPE // 2
    r1 = slice(QK_NOPE, QK_NOPE + half)
    r2 = slice(QK_NOPE + half, QK_NOPE + QK_ROPE)
    nt = (((1,), (1,)), ((), ()))
    x = x_ref[0]
    h = _rms(x) * gpre_ref[...] * (1.0 + scale_ref[0]) + shift_ref[0]
    hb = h.astype(_BF16)

    def proj(w_ref):
        return jnp.dot(hb, w_ref[...], preferred_element_type=_F32)

    u_ref[0] = proj(w_u_ref)
    z = proj(w_zs_ref)
    szs_ref[0] = (z * _sigmoid(z)).astype(_BF16)
    z = proj(w_zm_ref)
    szm_ref[0] = (z * _sigmoid(z)).astype(_BF16)
    sgs_ref[0] = _sigmoid(proj(w_gs_ref)).astype(_BF16)
    sgm_ref[0] = _sigmoid(proj(w_gm_ref)).astype(_BF16)

    lat = proj(w_lat_ref)
    qn = (_rms(lat[:, :Q_LORA]) * gq_ref[...]).astype(_BF16)
    kvn = (_rms(lat[:, Q_LORA:]) * gkv_ref[...]).astype(_BF16)

    ang = invf_ref[...] * pos_ref[0].astype(_F32)
    cos = jnp.cos(ang)
    sin = jnp.sin(ang)

    qt = lax.dot_general(wqt_ref[...], qn, nt, preferred_element_type=_F32)
    for hd in range(MLA_HEADS):
        blk = qt[hd * HEAD_PAD:(hd + 1) * HEAD_PAD]
        t1, t2 = blk[r1], blk[r2]
        blk = jnp.concatenate([blk[:QK_NOPE], t1 * cos - t2 * sin, t2 * cos + t1 * sin,
                               blk[QK_NOPE + QK_ROPE:]], axis=0)
        qt_ref[0, 0, hd * HEAD_PAD:(hd + 1) * HEAD_PAD, :] = (blk * _SOFTMAX_SCALE_LOG2E).astype(_BF16)

    krt = lax.dot_general(w_krt_ref[...], hb, nt, preferred_element_type=_F32)
    t1, t2 = krt[:half], krt[half:QK_ROPE]
    spare = HEAD_PAD - QK_NOPE - QK_ROPE
    row = lax.broadcasted_iota(jnp.int32, (spare, tm), 0)
    col_chunk = lax.broadcasted_iota(jnp.int32, (spare, tm), 1) // CHUNK
    onehot = jnp.where(row == col_chunk, 1.0, 0.0)
    kr_full = jnp.concatenate([jnp.zeros((QK_NOPE, tm), _F32), t1 * cos - t2 * sin,
                               t2 * cos + t1 * sin, onehot], axis=0)
    kr_place = jnp.transpose(kr_full)
    ka = jnp.dot(kvn, wk_ref[...], preferred_element_type=_F32)
    for hd in range(MLA_HEADS):
        k_ref[0, hd] = (ka[:, hd * HEAD_PAD:(hd + 1) * HEAD_PAD] + kr_place).astype(_BF16)
    vt_ref[0, 0] = lax.dot_general(wvt_ref[...], kvn, nt, preferred_element_type=_F32).astype(_BF16)


def _inproj(x, shift, scale, g_pre, pos3, invf, ws, tm):
    bsz, seq, _ = x.shape
    tok = lambda w: pl.BlockSpec((1, tm, w), lambda b, i: (b, i, 0))
    per_b = pl.BlockSpec((1, 1, D_MODEL), lambda b, i: (b, 0, 0))
    head = pl.BlockSpec((1, MLA_HEADS, tm, HEAD_PAD), lambda b, i: (b, 0, i, 0))
    tiled = lambda rows: pl.BlockSpec((1, 1, rows, tm), lambda b, i: (b, i, 0, 0))
    tshape = lambda rows: jax.ShapeDtypeStruct((bsz, seq // tm, rows, tm), _BF16)
    act = jax.ShapeDtypeStruct((bsz, seq, D_MODEL), _BF16)
    w_specs = [_const_spec(w.shape) for w in ws]
    return pl.pallas_call(
        _inproj_kernel,
        grid=(bsz, seq // tm),
        in_specs=[tok(D_MODEL), per_b, per_b, _const_spec((1, D_MODEL)),
                  pl.BlockSpec((1, 1, tm), lambda b, i: (b, 0, i)),
                  _const_spec((QK_ROPE // 2, 1))] + w_specs,
        out_specs=[tok(D_MODEL)] * 5 + [tiled(MLA_HEADS * HEAD_PAD), head, tiled(MLA_WIDTH)],
        out_shape=[jax.ShapeDtypeStruct((bsz, seq, SSM_WIDTH), _F32)] + [act] * 4
                  + [tshape(MLA_HEADS * HEAD_PAD),
                     jax.ShapeDtypeStruct((bsz, MLA_HEADS, seq, HEAD_PAD), _BF16), tshape(MLA_WIDTH)],
        compiler_params=pltpu.CompilerParams(dimension_semantics=("parallel", "parallel"),
                                             vmem_limit_bytes=VMEM_LIMIT),
        name="inproj",
    )(x, shift, scale, g_pre, pos3, invf, *ws)


def _ssm_weights(log_dt, a_re, a_im, b_re, b_im, c_re, c_im, d_skip):
    hi = lax.Precision.HIGHEST
    L, G, N, P = SSM_SUB, SSM_GROUPS, SSM_STATE, SSM_GROUP
    nb, gb = G // SSM_HALF, SSM_HALF
    flat = L * gb * P
    dt = jnp.exp(log_dt.astype(_F32))[:, None]
    lr, li = a_re.astype(_F32), a_im.astype(_F32)
    mag = jnp.exp(lr * dt)
    abar_re, abar_im = mag * jnp.cos(li * dt), mag * jnp.sin(li * dt)
    den = lr * lr + li * li
    nr, ni = abar_re - 1.0, abar_im
    fr = (nr * lr + ni * li) / den
    fi = (ni * lr - nr * li) / den
    br, bi = b_re.astype(_F32), b_im.astype(_F32)
    bbr = fr[..., None] * br - fi[..., None] * bi
    bbi = fr[..., None] * bi + fi[..., None] * br
    cr, ci = c_re.astype(_F32), c_im.astype(_F32)
    j = jnp.arange(L + 1, dtype=_F32)[:, None, None]
    pmag = jnp.exp(lr * dt * j)
    pr, pi_ = pmag * jnp.cos(li * dt * j), pmag * jnp.sin(li * dt * j)
    abr = pr[:L, :, :, None] * bbr - pi_[:L, :, :, None] * bbi
    abi = pr[:L, :, :, None] * bbi + pi_[:L, :, :, None] * bbr
    kj = (jnp.einsum('gpn,jgnq->jgpq', cr, abr, precision=hi)
          - jnp.einsum('gpn,jgnq->jgpq', ci, abi, precision=hi))
    kj = kj.at[0].add(jnp.eye(P, dtype=_F32) * d_skip.astype(_F32).reshape(G, P)[:, :, None])
    lag = jnp.arange(L)[None, :] - jnp.arange(L)[:, None]
    toe = jnp.where((lag >= 0)[:, :, None, None, None], kj[jnp.clip(lag, 0)], 0.0)
    toe = toe.reshape(L, L, nb, gb, P, P)
    t_c = jnp.transpose(toe, (2, 0, 3, 5, 1, 4)).reshape(nb, flat, L * P)
    ab = jnp.stack([abr[::-1], abi[::-1]], axis=0).reshape(2, L, nb, gb, N, P)
    s_c = jnp.transpose(ab, (2, 1, 3, 5, 0, 4)).reshape(nb, flat, 2 * N)
    er = cr[None] * pr[1:, :, None, :] - ci[None] * pi_[1:, :, None, :]
    ei = cr[None] * pi_[1:, :, None, :] + ci[None] * pr[1:, :, None, :]
    e = jnp.stack([er, -ei], axis=0).reshape(2, L, nb, gb, P, N)
    g_c = jnp.transpose(e, (2, 0, 3, 5, 1, 4)).reshape(nb, 2 * gb * N, L * P)

    def spread(compact, col_block, row_block):
        width = compact.shape[-1] * gb
        lane = np.arange(width)
        src = (lane // (gb * col_block)) * col_block + lane % col_block
        onehot = jnp.asarray(np.arange(compact.shape[-1])[:, None] == src[None, :], _BF16)
        full = jnp.einsum('brk,kc->brc', compact.astype(_BF16), onehot, preferred_element_type=_F32)
        row_g = (np.arange(compact.shape[1]) // row_block) % gb
        col_g = (lane // col_block) % gb
        return jnp.where(jnp.asarray(row_g[:, None] == col_g[None, :]), full, 0.0).astype(_BF16)

    w_t = spread(t_c, P, P)
    w_s = spread(s_c, N, P)
    w_g = spread(g_c, P, N)
    a_r = pr[L].reshape(SSM_NB, 1, SSM_NS)
    a_i = pi_[L].reshape(SSM_NB, 1, SSM_NS)
    return w_t, w_s, w_g, a_r, a_i


def _ssm_kernel(u_ref, wt_ref, ws_ref, wg_ref, ar_ref, ai_ref, y_ref, lhs_scr, v_scr, yp_scr, st_scr):
    bsz, sblk, _ = u_ref.shape
    nsub = sblk // SSM_SUB
    ns = SSM_NS
    hl = SSM_HALF * SSM_GROUP
    hs = SSM_HALF * SSM_STATE
    hf = SSM_SUB * hl

    @pl.when(pl.program_id(1) == 0)
    def _():
        st_scr[...] = jnp.zeros_like(st_scr)

    for b in range(bsz):
        for s in range(SSM_SUB):
            lhs_scr[s, pl.ds(b, nsub, stride=bsz), :] = u_ref[b, pl.ds(s, nsub, stride=SSM_SUB), :]
    slabs = [lhs_scr[s] for s in range(SSM_SUB)]
    lhs = [jnp.concatenate([x[:, h * hl:(h + 1) * hl] for x in slabs], axis=-1).astype(_BF16)
           for h in range(2)]
    for h in range(2):
        v = jnp.dot(lhs[h], ws_ref[h], preferred_element_type=_F32)
        v_scr[:, h * hs:(h + 1) * hs] = v[:, :hs]
        v_scr[:, ns + h * hs:ns + (h + 1) * hs] = v[:, hs:]

    ar = jnp.broadcast_to(ar_ref[0], (bsz, ns))
    ai = jnp.broadcast_to(ai_ref[0], (bsz, ns))

    def step(c, carry):
        sr, si = carry
        r0 = pl.multiple_of(c * bsz, bsz)
        vr = v_scr[pl.ds(r0, bsz), 0:ns]
        vi = v_scr[pl.ds(r0, bsz), ns:2 * ns]
        v_scr[pl.ds(r0, bsz), 0:ns] = sr
        v_scr[pl.ds(r0, bsz), ns:2 * ns] = si
        return ar * sr - ai * si + vr, ar * si + ai * sr + vi

    sr, si = lax.fori_loop(0, nsub, step, (st_scr[0], st_scr[1]), unroll=4)
    st_scr[0] = sr
    st_scr[1] = si

    half_f = hf // 2
    yp = []
    for h in range(2):
        s_prev = jnp.concatenate([v_scr[:, h * hs:(h + 1) * hs],
                                  v_scr[:, ns + h * hs:ns + (h + 1) * hs]], axis=-1).astype(_BF16)
        carry_in = jnp.dot(s_prev, wg_ref[h], preferred_element_type=_F32)
        lo = carry_in[:, :half_f] + jnp.dot(lhs[h][:, :half_f], wt_ref[h, :half_f, :half_f],
                                            preferred_element_type=_F32)
        hi = carry_in[:, half_f:] + jnp.dot(lhs[h], wt_ref[h, :, half_f:], preferred_element_type=_F32)
        yp.append(jnp.concatenate([lo, hi], axis=-1))
    for t in range(SSM_SUB):
        yp_scr[t] = jnp.concatenate([yp[h][:, t * hl:(t + 1) * hl] for h in range(2)], axis=-1)
    for b in range(bsz):
        for t in range(SSM_SUB):
            y_ref[b, pl.ds(t, nsub, stride=SSM_SUB), :] = yp_scr[t, pl.ds(b, nsub, stride=bsz), :]


def _ssm(u, w_t, w_s, w_g, a_r, a_i, sblk):
    bsz, seq, _ = u.shape
    rows = (sblk // SSM_SUB) * bsz
    hf = SSM_SUB * SSM_HALF * SSM_GROUP
    tok = pl.BlockSpec((bsz, sblk, LANES), lambda g, i: (0, i, g))
    wspec = pl.BlockSpec((2, hf, hf), lambda g, i: (g, 0, 0))
    aspec = pl.BlockSpec((1, 1, SSM_NS), lambda g, i: (g, 0, 0))
    return pl.pallas_call(
        _ssm_kernel,
        grid=(SSM_NB, seq // sblk),
        in_specs=[tok, wspec, wspec, wspec, aspec, aspec],
        out_specs=tok,
        out_shape=jax.ShapeDtypeStruct(u.shape, _F32),
        scratch_shapes=[pltpu.VMEM((SSM_SUB, rows, LANES), _F32),
                        pltpu.VMEM((rows, 2 * SSM_NS), _F32),
                        pltpu.VMEM((SSM_SUB, rows, LANES), _F32),
                        pltpu.VMEM((2, bsz, SSM_NS), _F32)],
        compiler_params=pltpu.CompilerParams(dimension_semantics=("parallel", "arbitrary"),
                                             vmem_limit_bytes=VMEM_LIMIT),
        name="ssm",
    )(u, w_t, w_s, w_g, a_r, a_i)


def _attn_kernel(qt_ref, k_ref, vt_ref, o_ref, s_ref, acc_ref, pen_ref):
    t = ATT_TILE
    nq = qt_ref.shape[1]
    nh = k_ref.shape[1]
    ones = jnp.ones((ONES_ROWS, t), _BF16)

    row = lax.broadcasted_iota(jnp.int32, (HEAD_PAD, t), 0) - (QK_NOPE + QK_ROPE)
    qry_chunk = lax.broadcasted_iota(jnp.int32, (HEAD_PAD, t), 1) // CHUNK
    pen_ref[0] = jnp.zeros((HEAD_PAD, t), _BF16)
    pen_ref[1] = jnp.where((row >= 0) & (row < t // CHUNK) & (row > qry_chunk), _NEG_BIG, 0.0).astype(_BF16)

    def scores(dst_ref, qi, kj):
        k0 = pl.multiple_of(kj * t, t)
        pen = pen_ref[(kj == qi).astype(jnp.int32)]
        tile_max = []
        for hh in range(nh):
            qt = qt_ref[0, qi, hh * HEAD_PAD:(hh + 1) * HEAD_PAD, :] + pen
            st = jnp.dot(k_ref[0, hh, pl.ds(k0, t), :], qt, preferred_element_type=_F32)
            dst_ref[hh] = st
            tile_max.append(jnp.max(st, axis=0, keepdims=True))
        return tuple(tile_max)

    def step(cur_ref, nxt_ref, carry):
        qi, kj, tmax, ms, accs = carry
        last = kj == qi
        kj_n = jnp.where(last, 0, kj + 1)
        qi_n = jnp.where(last, qi + 1, qi)
        tmax_n = scores(nxt_ref, jnp.minimum(qi_n, nq - 1), kj_n)
        ms_n, accs_n = [], []
        for hh in range(nh):
            m_new = jnp.maximum(ms[hh], tmax[hh])
            alpha = jnp.exp2(ms[hh] - m_new)
            p = jnp.exp2(cur_ref[hh] - m_new).astype(_BF16)
            vte = jnp.concatenate([vt_ref[0, kj, hh * V_HEAD:(hh + 1) * V_HEAD, :], ones], axis=0)
            acc = alpha * accs[hh] + jnp.dot(vte, p, preferred_element_type=_F32)
            acc_ref[qi, hh] = acc
            ms_n.append(jnp.where(last, _NEG_BIG, m_new))
            accs_n.append(jnp.where(last, 0.0, acc))
        return qi_n, kj_n, tmax_n, tuple(ms_n), tuple(accs_n)

    zero = jnp.int32(0)
    bufs = [s_ref.at[i] for i in range(ATT_BUFS)]
    init = (zero, zero, scores(bufs[0], zero, zero),
            tuple(jnp.full((1, t), _NEG_BIG, _F32) for _ in range(nh)),
            tuple(jnp.zeros((V_HEAD + ONES_ROWS, t), _F32) for _ in range(nh)))
    nsteps = nq * (nq + 1) // 2

    def run(count, c):
        for i in range(count):
            c = step(bufs[i % ATT_BUFS], bufs[(i + 1) % ATT_BUFS], c)
        return c

    assert ATT_UNROLL % ATT_BUFS == 0
    carry = lax.fori_loop(0, nsteps // ATT_UNROLL, lambda _, c: run(ATT_UNROLL, c), init)
    run(nsteps % ATT_UNROLL, carry)

    def finish(qi, _):
        outs = [acc_ref[qi, hh, :V_HEAD] / acc_ref[qi, hh, V_HEAD:V_HEAD + 1] for hh in range(nh)]
        o_ref[0, pl.ds(pl.multiple_of(qi * t, t), t), :] = (
            jnp.transpose(jnp.concatenate(outs, axis=0)).astype(_BF16))
        return 0

    lax.fori_loop(0, nq, finish, 0)


def _attn(qt, k, vt):
    bsz, heads, seq, _ = k.shape
    nh = ATT_HEADS
    k_spec = pl.BlockSpec((1, nh, seq, HEAD_PAD), lambda b, h: (b, h, 0, 0))
    tiled = lambda rows: pl.BlockSpec((1, seq // ATT_TILE, nh * rows, ATT_TILE), lambda b, h: (b, 0, h, 0))
    return pl.pallas_call(
        _attn_kernel,
        grid=(bsz, heads // nh),
        in_specs=[tiled(HEAD_PAD), k_spec, tiled(V_HEAD)],
        out_specs=pl.BlockSpec((1, seq, nh * V_HEAD), lambda b, h: (b, 0, h)),
        out_shape=jax.ShapeDtypeStruct((bsz, seq, MLA_WIDTH), _BF16),
        scratch_shapes=[pltpu.VMEM((ATT_BUFS, nh, ATT_TILE, ATT_TILE), _F32),
                        pltpu.VMEM((seq // ATT_TILE, nh, V_HEAD + ONES_ROWS, ATT_TILE), _F32),
                        pltpu.VMEM((2, HEAD_PAD, ATT_TILE), _BF16)],
        compiler_params=pltpu.CompilerParams(dimension_semantics=("parallel", "parallel"),
                                             vmem_limit_bytes=VMEM_LIMIT),
        name="attn",
    )(qt, k, vt)


def _post_kernel(x_ref, gate_ref, y_ref, szs_ref, sgs_ref, o_ref, szm_ref, sgm_ref,
                 wglu_ref, bglu_ref, wbs_ref, wbm_ref, wout_ref, gpost_ref, out_ref):
    ge = _gelu_tanh(y_ref[0]).astype(_BF16)
    gl = jnp.dot(ge, wglu_ref[...], preferred_element_type=_F32) + bglu_ref[...]
    a = (gl[:, :SSM_WIDTH] * _sigmoid(gl[:, SSM_WIDTH:])) * szs_ref[0].astype(_F32)
    ys = jnp.dot(a.astype(_BF16), wbs_ref[...], preferred_element_type=_F32)
    am = o_ref[0].astype(_F32) * szm_ref[0].astype(_F32)
    ym = jnp.dot(am.astype(_BF16), wbm_ref[...], preferred_element_type=_F32)
    merged = sgs_ref[0].astype(_F32) * ys + sgm_ref[0].astype(_F32) * ym
    out = jnp.dot(merged.astype(_BF16), wout_ref[...], preferred_element_type=_F32)
    out_ref[0] = x_ref[0] + gate_ref[0] * (_rms(out) * gpost_ref[...])


def _post(x, gate, y, szs, sgs, o, szm, sgm, wglu, bglu, wbs, wbm, wout, gpost, tm):
    bsz, seq, _ = x.shape
    tok = pl.BlockSpec((1, tm, D_MODEL), lambda b, i: (b, i, 0))
    per_b = pl.BlockSpec((1, 1, D_MODEL), lambda b, i: (b, 0, 0))
    consts = [wglu, bglu, wbs, wbm, wout, gpost]
    return pl.pallas_call(
        _post_kernel,
        grid=(bsz, seq // tm),
        in_specs=[tok, per_b] + [tok] * 6 + [_const_spec(w.shape) for w in consts],
        out_specs=tok,
        out_shape=jax.ShapeDtypeStruct(x.shape, x.dtype),
        compiler_params=pltpu.CompilerParams(dimension_semantics=("parallel", "parallel"),
                                             vmem_limit_bytes=VMEM_LIMIT),
        name="post",
    )(x, gate, y, szs, sgs, o, szm, sgm, *consts)


def _mla_weights(w_q_up, w_kv_up):
    H = MLA_HEADS
    wq = w_q_up.reshape(Q_LORA, H, QK_NOPE + QK_ROPE)
    zq = jnp.zeros((Q_LORA, H, HEAD_PAD - QK_NOPE - QK_ROPE), w_q_up.dtype)
    wq_t = jnp.transpose(jnp.concatenate([wq, zq], axis=-1).reshape(Q_LORA, H * HEAD_PAD))
    wkv = w_kv_up.reshape(KV_LORA, H, QK_NOPE + V_HEAD)
    zk = jnp.zeros((KV_LORA, H, HEAD_PAD - QK_NOPE), w_kv_up.dtype)
    wk_main = jnp.concatenate([wkv[..., :QK_NOPE], zk], axis=-1).reshape(KV_LORA, H * HEAD_PAD)
    wv_t = jnp.transpose(wkv[..., QK_NOPE:].reshape(KV_LORA, H * V_HEAD))
    return [w.astype(_BF16) for w in (wq_t, wk_main, wv_t)]


def _rope_inv_freq_col():
    inv_freq = ROPE_BASE ** (-jnp.arange(0, QK_ROPE, 2, dtype=_F32) / QK_ROPE)
    return inv_freq.reshape(QK_ROPE // 2, 1)


def kernel(x, c, positions, w_ada, b_ada, g_pre, w_in, ssm_log_dt, ssm_a_re, ssm_a_im, ssm_b_re, ssm_b_im, ssm_c_re, ssm_c_im, ssm_d, w_glu, b_glu, g_q_norm, w_q_up, g_kv_norm, w_kv_up, w_br_ssm, w_br_mla, w_out, g_post):
    bsz, seq, _ = x.shape
    depth = w_ada.shape[0]
    assert bsz == SUBLANES and seq % ATT_TILE == 0
    tm = ATT_TILE
    sblk = min(512, seq)
    pos3 = positions.reshape(bsz, 1, seq)
    invf = _rope_inv_freq_col()
    off = np.cumsum((0, SSM_WIDTH, SSM_WIDTH, Q_LORA, KV_LORA, QK_ROPE, MLA_WIDTH, D_MODEL, D_MODEL))

    for l in range(depth):
        mod = _ada(c, w_ada[l], b_ada[l])
        shift, scale, gate = (mod[:, None, i * D_MODEL:(i + 1) * D_MODEL] for i in range(3))

        wb = w_in[l].astype(_BF16)
        seg = lambda i: wb[:, off[i]:off[i + 1]]
        w_lat = jnp.concatenate([seg(2), seg(3)], axis=1)
        w_krt = jnp.concatenate(
            [jnp.transpose(seg(4)), jnp.zeros((HEAD_PAD - QK_ROPE, D_MODEL), _BF16)], axis=0)
        ws = [seg(0), seg(1), seg(5), seg(6), seg(7), w_lat, w_krt,
              g_q_norm[l].reshape(1, Q_LORA), g_kv_norm[l].reshape(1, KV_LORA)]
        ws += _mla_weights(w_q_up[l], w_kv_up[l])
        u, szs, szm, sgs, sgm, qt, k, vt = _inproj(
            x, shift, scale, g_pre[l].reshape(1, D_MODEL), pos3, invf, ws, tm)

        w_t, w_s, w_g, a_r, a_i = _ssm_weights(
            ssm_log_dt[l], ssm_a_re[l], ssm_a_im[l], ssm_b_re[l], ssm_b_im[l],
            ssm_c_re[l], ssm_c_im[l], ssm_d[l])
        y = _ssm(u, w_t, w_s, w_g, a_r, a_i, sblk)

        o = _attn(qt, k, vt)

        x = _post(x, gate, y, szs, sgs, o, szm, sgm,
                  w_glu[l].astype(_BF16), b_glu[l].reshape(1, -1), w_br_ssm[l].astype(_BF16),
                  w_br_mla[l].astype(_BF16), w_out[l].astype(_BF16), g_post[l].reshape(1, D_MODEL),
                  min(POST_TILE, seq))
    return x
```

```python
import functools
import math

import jax
import jax.numpy as jnp
import numpy as np
from jax import lax
from jax.experimental import pallas as pl
from jax.experimental.pallas import tpu as pltpu

D_MODEL = 1024
CHUNK = 64
SSM_WIDTH = 1024
SSM_GROUP = 16
SSM_GROUPS = SSM_WIDTH // SSM_GROUP
SSM_STATE = 64
MLA_HEADS = 16
QK_NOPE = 64
QK_ROPE = 32
V_HEAD = 64
Q_LORA = 256
KV_LORA = 256
MLA_WIDTH = MLA_HEADS * V_HEAD
ROPE_BASE = 10000.0
EPS = 1e-6

LANES = 128
SUBLANES = 8
HEAD_PAD = 128
SSM_SUB = 8
SSM_BUNDLE = LANES // SSM_GROUP
SSM_NB = SSM_GROUPS // SSM_BUNDLE
SSM_HALF = SSM_BUNDLE // 2
SSM_NS = SSM_BUNDLE * SSM_STATE
ATT_TILE = 256
ATT_HEADS = 4
ATT_UNROLL = 8
ATT_BUFS = 2
POST_TILE = 512
ONES_ROWS = 16
VMEM_LIMIT = 56 * 1024 * 1024

_F32 = jnp.float32
_BF16 = jnp.bfloat16
_SOFTMAX_SCALE_LOG2E = ((QK_NOPE + QK_ROPE) ** -0.5) * math.log2(math.e)
_NEG_BIG = -1e30


def _const_spec(shape):
    zeros = (0,) * len(shape)
    return pl.BlockSpec(shape, lambda *_: zeros, pipeline_mode=pl.Buffered(1))


def _sigmoid(x):
    return 1.0 / (1.0 + jnp.exp(-x))


def _gelu_tanh(x):
    return 0.5 * x * (1.0 + jnp.tanh(math.sqrt(2.0 / math.pi) * (x + 0.044715 * (x * x * x))))


def _rms(x):
    return x * lax.rsqrt(jnp.mean(x * x, axis=-1, keepdims=True) + EPS)


def _ada_kernel(c_ref, w_ref, b_ref, o_ref):
    o_ref[...] = jnp.dot(c_ref[...], w_ref[...], preferred_element_type=_F32) + b_ref[...]


def _ada(c, w, b):
    bsz = c.shape[0]
    n = w.shape[1]
    bn = D_MODEL
    return pl.pallas_call(
        _ada_kernel,
        grid=(n // bn,),
        in_specs=[pl.BlockSpec((bsz, D_MODEL), lambda j: (0, 0)),
                  pl.BlockSpec((D_MODEL, bn), lambda j: (0, j)),
                  pl.BlockSpec((1, bn), lambda j: (0, j))],
        out_specs=pl.BlockSpec((bsz, bn), lambda j: (0, j)),
        out_shape=jax.ShapeDtypeStruct((bsz, n), _F32),
        compiler_params=pltpu.CompilerParams(dimension_semantics=("arbitrary",),
                                             vmem_limit_bytes=VMEM_LIMIT),
        name="ada",
    )(c, w, b.reshape(1, n))


def _inproj_kernel(x_ref, shift_ref, scale_ref, gpre_ref, pos_ref, invf_ref,
                   w_u_ref, w_zs_ref, w_zm_ref, w_gs_ref, w_gm_ref, w_lat_ref, w_krt_ref,
                   gq_ref, gkv_ref, wqt_ref, wk_ref, wvt_ref,
                   u_ref, szs_ref, szm_ref, sgs_ref, sgm_ref, qt_ref, k_ref, vt_ref):
    tm = x_ref.shape[1]
    half = QK_ROPE // 2
    r1 = slice(QK_NOPE, QK_NOPE + half)
    r2 = slice(QK_NOPE + half, QK_NOPE + QK_ROPE)
    nt = (((1,), (1,)), ((), ()))
    x = x_ref[0]
    h = _rms(x) * gpre_ref[...] * (1.0 + scale_ref[0]) + shift_ref[0]
    hb = h.astype(_BF16)

    def proj(w_ref):
        return jnp.dot(hb, w_ref[...], preferred_element_type=_F32)

    u_ref[0] = proj(w_u_ref)
    z = proj(w_zs_ref)
    szs_ref[0] = (z * _sigmoid(z)).astype(_BF16)
    z = proj(w_zm_ref)
    szm_ref[0] = (z * _sigmoid(z)).astype(_BF16)
    sgs_ref[0] = _sigmoid(proj(w_gs_ref)).astype(_BF16)
    sgm_ref[0] = _sigmoid(proj(w_gm_ref)).astype(_BF16)

    lat = proj(w_lat_ref)
    qn = (_rms(lat[:, :Q_LORA]) * gq_ref[...]).astype(_BF16)
    kvn = (_rms(lat[:, Q_LORA:]) * gkv_ref[...]).astype(_BF16)

    ang = invf_ref[...] * pos_ref[0].astype(_F32)
    cos = jnp.cos(ang)
    sin = jnp.sin(ang)

    qt = lax.dot_general(wqt_ref[...], qn, nt, preferred_element_type=_F32)
    for hd in range(MLA_HEADS):
        blk = qt[hd * HEAD_PAD:(hd + 1) * HEAD_PAD]
        t1, t2 = blk[r1], blk[r2]
        blk = jnp.concatenate([blk[:QK_NOPE], t1 * cos - t2 * sin, t2 * cos + t1 * sin,
                               blk[QK_NOPE + QK_ROPE:]], axis=0)
        qt_ref[0, 0, hd * HEAD_PAD:(hd + 1) * HEAD_PAD, :] = (blk * _SOFTMAX_SCALE_LOG2E).astype(_BF16)

    krt = lax.dot_general(w_krt_ref[...], hb, nt, preferred_element_type=_F32)
    t1, t2 = krt[:half], krt[half:QK_ROPE]
    spare = HEAD_PAD - QK_NOPE - QK_ROPE
    row = lax.broadcasted_iota(jnp.int32, (spare, tm), 0)
    col_chunk = lax.broadcasted_iota(jnp.int32, (spare, tm), 1) // CHUNK
    onehot = jnp.where(row == col_chunk, 1.0, 0.0)
    kr_full = jnp.concatenate([jnp.zeros((QK_NOPE, tm), _F32), t1 * cos - t2 * sin,
                               t2 * cos + t1 * sin, onehot], axis=0)
    kr_place = jnp.transpose(kr_full)
    ka = jnp.dot(kvn, wk_ref[...], preferred_element_type=_F32)
    for hd in range(MLA_HEADS):
        k_ref[0, hd] = (ka[:, hd * HEAD_PAD:(hd + 1) * HEAD_PAD] + kr_place).astype(_BF16)
    vt_ref[0, 0] = lax.dot_general(wvt_ref[...], kvn, nt, preferred_element_type=_F32).astype(_BF16)


def _inproj(x, shift, scale, g_pre, pos3, invf, ws, tm):
    bsz, seq, _ = x.shape
    tok = lambda w: pl.BlockSpec((1, tm, w), lambda b, i: (b, i, 0))
    per_b = pl.BlockSpec((1, 1, D_MODEL), lambda b, i: (b, 0, 0))
    head = pl.BlockSpec((1, MLA_HEADS, tm, HEAD_PAD), lambda b, i: (b, 0, i, 0))
    tiled = lambda rows: pl.BlockSpec((1, 1, rows, tm), lambda b, i: (b, i, 0, 0))
    tshape = lambda rows: jax.ShapeDtypeStruct((bsz, seq // tm, rows, tm), _BF16)
    act = jax.ShapeDtypeStruct((bsz, seq, D_MODEL), _BF16)
    w_specs = [_const_spec(w.shape) for w in ws]
    return pl.pallas_call(
        _inproj_kernel,
        grid=(bsz, seq // tm),
        in_specs=[tok(D_MODEL), per_b, per_b, _const_spec((1, D_MODEL)),
                  pl.BlockSpec((1, 1, tm), lambda b, i: (b, 0, i)),
                  _const_spec((QK_ROPE // 2, 1))] + w_specs,
        out_specs=[tok(D_MODEL)] * 5 + [tiled(MLA_HEADS * HEAD_PAD), head, tiled(MLA_WIDTH)],
        out_shape=[jax.ShapeDtypeStruct((bsz, seq, SSM_WIDTH), _F32)] + [act] * 4
                  + [tshape(MLA_HEADS * HEAD_PAD),
                     jax.ShapeDtypeStruct((bsz, MLA_HEADS, seq, HEAD_PAD), _BF16), tshape(MLA_WIDTH)],
        compiler_params=pltpu.CompilerParams(dimension_semantics=("parallel", "parallel"),
                                             vmem_limit_bytes=VMEM_LIMIT),
        name="inproj",
    )(x, shift, scale, g_pre, pos3, invf, *ws)


def _ssm_weights(log_dt, a_re, a_im, b_re, b_im, c_re, c_im, d_skip):
    hi = lax.Precision.HIGHEST
    L, G, N, P = SSM_SUB, SSM_GROUPS, SSM_STATE, SSM_GROUP
    nb, gb = G // SSM_HALF, SSM_HALF
    flat = L * gb * P
    dt = jnp.exp(log_dt.astype(_F32))[:, None]
    lr, li = a_re.astype(_F32), a_im.astype(_F32)
    mag = jnp.exp(lr * dt)
    abar_re, abar_im = mag * jnp.cos(li * dt), mag * jnp.sin(li * dt)
    den = lr * lr + li * li
    nr, ni = abar_re - 1.0, abar_im
    fr = (nr * lr + ni * li) / den
    fi = (ni * lr - nr * li) / den
    br, bi = b_re.astype(_F32), b_im.astype(_F32)
    bbr = fr[..., None] * br - fi[..., None] * bi
    bbi = fr[..., None] * bi + fi[..., None] * br
    cr, ci = c_re.astype(_F32), c_im.astype(_F32)
    j = jnp.arange(L + 1, dtype=_F32)[:, None, None]
    pmag = jnp.exp(lr * dt * j)
    pr, pi_ = pmag * jnp.cos(li * dt * j), pmag * jnp.sin(li * dt * j)
    abr = pr[:L, :, :, None] * bbr - pi_[:L, :, :, None] * bbi
    abi = pr[:L, :, :, None] * bbi + pi_[:L, :, :, None] * bbr
    kj = (jnp.einsum('gpn,jgnq->jgpq', cr, abr, precision=hi)
          - jnp.einsum('gpn,jgnq->jgpq', ci, abi, precision=hi))
    kj = kj.at[0].add(jnp.eye(P, dtype=_F32) * d_skip.astype(_F32).reshape(G, P)[:, :, None])
    lag = jnp.arange(L)[None, :] - jnp.arange(L)[:, None]
    toe = jnp.where((lag >= 0)[:, :, None, None, None], kj[jnp.clip(lag, 0)], 0.0)
    toe = toe.reshape(L, L, nb, gb, P, P)
    t_c = jnp.transpose(toe, (2, 0, 3, 5, 1, 4)).reshape(nb, flat, L * P)
    ab = jnp.stack([abr[::-1], abi[::-1]], axis=0).reshape(2, L, nb, gb, N, P)
    s_c = jnp.transpose(ab, (2, 1, 3, 5, 0, 4)).reshape(nb, flat, 2 * N)
    er = cr[None] * pr[1:, :, None, :] - ci[None] * pi_[1:, :, None, :]
    ei = cr[None] * pi_[1:, :, None, :] + ci[None] * pr[1:, :, None, :]
    e = jnp.stack([er, -ei], axis=0).reshape(2, L, nb, gb, P, N)
    g_c = jnp.transpose(e, (2, 0, 3, 5, 1, 4)).reshape(nb, 2 * gb * N, L * P)

    def spread(compact, col_block, row_block):
        width = compact.shape[-1] * gb
        lane = np.arange(width)
        src = (lane // (gb * col_block)) * col_block + lane % col_block
        onehot = jnp.asarray(np.arange(compact.shape[-1])[:, None] == src[None, :], _BF16)
        full = jnp.einsum('brk,kc->brc', compact.astype(_BF16), onehot, preferred_element_type=_F32)
        row_g = (np.arange(compact.shape[1]) // row_block) % gb
        col_g = (lane // col_block) % gb
        return jnp.where(jnp.asarray(row_g[:, None] == col_g[None, :]), full, 0.0).astype(_BF16)

    w_t = spread(t_c, P, P)
    w_s = spread(s_c, N, P)
    w_g = spread(g_c, P, N)
    a_r = pr[L].reshape(SSM_NB, 1, SSM_NS)
    a_i = pi_[L].reshape(SSM_NB, 1, SSM_NS)
    return w_t, w_s, w_g, a_r, a_i


def _ssm_kernel(u_ref, wt_ref, ws_ref, wg_ref, ar_ref, ai_ref, y_ref, lhs_scr, v_scr, yp_scr, st_scr):
    bsz, sblk, _ = u_ref.shape
    nsub = sblk // SSM_SUB
    ns = SSM_NS
    hl = SSM_HALF * SSM_GROUP
    hs = SSM_HALF * SSM_STATE
    hf = SSM_SUB * hl

    @pl.when(pl.program_id(1) == 0)
    def _():
        st_scr[...] = jnp.zeros_like(st_scr)

    for b in range(bsz):
        for s in range(SSM_SUB):
            lhs_scr[s, pl.ds(b, nsub, stride=bsz), :] = u_ref[b, pl.ds(s, nsub, stride=SSM_SUB), :]
    slabs = [lhs_scr[s] for s in range(SSM_SUB)]
    lhs = [jnp.concatenate([x[:, h * hl:(h + 1) * hl] for x in slabs], axis=-1).astype(_BF16)
           for h in range(2)]
    for h in range(2):
        v = jnp.dot(lhs[h], ws_ref[h], preferred_element_type=_F32)
        v_scr[:, h * hs:(h + 1) * hs] = v[:, :hs]
        v_scr[:, ns + h * hs:ns + (h + 1) * hs] = v[:, hs:]

    ar = jnp.broadcast_to(ar_ref[0], (bsz, ns))
    ai = jnp.broadcast_to(ai_ref[0], (bsz, ns))

    def step(c, carry):
        sr, si = carry
        r0 = pl.multiple_of(c * bsz, bsz)
        vr = v_scr[pl.ds(r0, bsz), 0:ns]
        vi = v_scr[pl.ds(r0, bsz), ns:2 * ns]
        v_scr[pl.ds(r0, bsz), 0:ns] = sr
        v_scr[pl.ds(r0, bsz), ns:2 * ns] = si
        return ar * sr - ai * si + vr, ar * si + ai * sr + vi

    sr, si = lax.fori_loop(0, nsub, step, (st_scr[0], st_scr[1]), unroll=4)
    st_scr[0] = sr
    st_scr[1] = si

    half_f = hf // 2
    yp = []
    for h in range(2):
        s_prev = jnp.concatenate([v_scr[:, h * hs:(h + 1) * hs],
                                  v_scr[:, ns + h * hs:ns + (h + 1) * hs]], axis=-1).astype(_BF16)
        carry_in = jnp.dot(s_prev, wg_ref[h], preferred_element_type=_F32)
        lo = carry_in[:, :half_f] + jnp.dot(lhs[h][:, :half_f], wt_ref[h, :half_f, :half_f],
                                            preferred_element_type=_F32)
        hi = carry_in[:, half_f:] + jnp.dot(lhs[h], wt_ref[h, :, half_f:], preferred_element_type=_F32)
        yp.append(jnp.concatenate([lo, hi], axis=-1))
    for t in range(SSM_SUB):
        yp_scr[t] = jnp.concatenate([yp[h][:, t * hl:(t + 1) * hl] for h in range(2)], axis=-1)
    for b in range(bsz):
        for t in range(SSM_SUB):
            y_ref[b, pl.ds(t, nsub, stride=SSM_SUB), :] = yp_scr[t, pl.ds(b, nsub, stride=bsz), :]


def _ssm(u, w_t, w_s, w_g, a_r, a_i, sblk):
    bsz, seq, _ = u.shape
    rows = (sblk // SSM_SUB) * bsz
    hf = SSM_SUB * SSM_HALF * SSM_GROUP
    tok = pl.BlockSpec((bsz, sblk, LANES), lambda g, i: (0, i, g))
    wspec = pl.BlockSpec((2, hf, hf), lambda g, i: (g, 0, 0))
    aspec = pl.BlockSpec((1, 1, SSM_NS), lambda g, i: (g, 0, 0))
    return pl.pallas_call(
        _ssm_kernel,
        grid=(SSM_NB, seq // sblk),
        in_specs=[tok, wspec, wspec, wspec, aspec, aspec],
        out_specs=tok,
        out_shape=jax.ShapeDtypeStruct(u.shape, _F32),
        scratch_shapes=[pltpu.VMEM((SSM_SUB, rows, LANES), _F32),
                        pltpu.VMEM((rows, 2 * SSM_NS), _F32),
                        pltpu.VMEM((SSM_SUB, rows, LANES), _F32),
                        pltpu.VMEM((2, bsz, SSM_NS), _F32)],
        compiler_params=pltpu.CompilerParams(dimension_semantics=("parallel", "arbitrary"),
                                             vmem_limit_bytes=VMEM_LIMIT),
        name="ssm",
    )(u, w_t, w_s, w_g, a_r, a_i)


def _attn_kernel(qt_ref, k_ref, vt_ref, o_ref, s_ref, acc_ref, pen_ref):
    t = ATT_TILE
    nq = qt_ref.shape[1]
    nh = k_ref.shape[1]
    ones = jnp.ones((ONES_ROWS, t), _BF16)

    row = lax.broadcasted_iota(jnp.int32, (HEAD_PAD, t), 0) - (QK_NOPE + QK_ROPE)
    qry_chunk = lax.broadcasted_iota(jnp.int32, (HEAD_PAD, t), 1) // CHUNK
    pen_ref[0] = jnp.zeros((HEAD_PAD, t), _BF16)
    pen_ref[1] = jnp.where((row >= 0) & (row < t // CHUNK) & (row > qry_chunk), _NEG_BIG, 0.0).astype(_BF16)

    def scores(dst_ref, qi, kj):
        k0 = pl.multiple_of(kj * t, t)
        pen = pen_ref[(kj == qi).astype(jnp.int32)]
        tile_max = []
        for hh in range(nh):
            qt = qt_ref[0, qi, hh * HEAD_PAD:(hh + 1) * HEAD_PAD, :] + pen
            st = jnp.dot(k_ref[0, hh, pl.ds(k0, t), :], qt, preferred_element_type=_F32)
            dst_ref[hh] = st
            tile_max.append(jnp.max(st, axis=0, keepdims=True))
        return tuple(tile_max)

    def step(cur_ref, nxt_ref, carry):
        qi, kj, tmax, ms, accs = carry
        last = kj == qi
        kj_n = jnp.where(last, 0, kj + 1)
        qi_n = jnp.where(last, qi + 1, qi)
        tmax_n = scores(nxt_ref, jnp.minimum(qi_n, nq - 1), kj_n)
        ms_n, accs_n = [], []
        for hh in range(nh):
            m_new = jnp.maximum(ms[hh], tmax[hh])
            alpha = jnp.exp2(ms[hh] - m_new)
            p = jnp.exp2((cur_ref[hh] - m_new).astype(_BF16))
            vte = jnp.concatenate([vt_ref[0, kj, hh * V_HEAD:(hh + 1) * V_HEAD, :], ones], axis=0)
            acc = alpha * accs[hh] + jnp.dot(vte, p, preferred_element_type=_F32)
            acc_ref[qi, hh] = acc
            ms_n.append(jnp.where(last, _NEG_BIG, m_new))
            accs_n.append(jnp.where(last, 0.0, acc))
        return qi_n, kj_n, tmax_n, tuple(ms_n), tuple(accs_n)

    zero = jnp.int32(0)
    bufs = [s_ref.at[i] for i in range(ATT_BUFS)]
    init = (zero, zero, scores(bufs[0], zero, zero),
            tuple(jnp.full((1, t), _NEG_BIG, _F32) for _ in range(nh)),
            tuple(jnp.zeros((V_HEAD + ONES_ROWS, t), _F32) for _ in range(nh)))
    nsteps = nq * (nq + 1) // 2

    def run(count, c):
        for i in range(count):
            c = step(bufs[i % ATT_BUFS], bufs[(i + 1) % ATT_BUFS], c)
        return c

    assert ATT_UNROLL % ATT_BUFS == 0
    carry = lax.fori_loop(0, nsteps // ATT_UNROLL, lambda _, c: run(ATT_UNROLL, c), init)
    run(nsteps % ATT_UNROLL, carry)

    def finish(qi, _):
        outs = [acc_ref[qi, hh, :V_HEAD] / acc_ref[qi, hh, V_HEAD:V_HEAD + 1] for hh in range(nh)]
        o_ref[0, pl.ds(pl.multiple_of(qi * t, t), t), :] = (
            jnp.transpose(jnp.concatenate(outs, axis=0)).astype(_BF16))
        return 0

    lax.fori_loop(0, nq, finish, 0)


def _attn(qt, k, vt):
    bsz, heads, seq, _ = k.shape
    nh = ATT_HEADS
    k_spec = pl.BlockSpec((1, nh, seq, HEAD_PAD), lambda b, h: (b, h, 0, 0))
    tiled = lambda rows: pl.BlockSpec((1, seq // ATT_TILE, nh * rows, ATT_TILE), lambda b, h: (b, 0, h, 0))
    return pl.pallas_call(
        _attn_kernel,
        grid=(bsz, heads // nh),
        in_specs=[tiled(HEAD_PAD), k_spec, tiled(V_HEAD)],
        out_specs=pl.BlockSpec((1, seq, nh * V_HEAD), lambda b, h: (b, 0, h)),
        out_shape=jax.ShapeDtypeStruct((bsz, seq, MLA_WIDTH), _BF16),
        scratch_shapes=[pltpu.VMEM((ATT_BUFS, nh, ATT_TILE, ATT_TILE), _F32),
                        pltpu.VMEM((seq // ATT_TILE, nh, V_HEAD + ONES_ROWS, ATT_TILE), _F32),
                        pltpu.VMEM((2, HEAD_PAD, ATT_TILE), _BF16)],
        compiler_params=pltpu.CompilerParams(dimension_semantics=("parallel", "parallel"),
                                             vmem_limit_bytes=VMEM_LIMIT),
        name="attn",
    )(qt, k, vt)


def _post_kernel(x_ref, gate_ref, y_ref, szs_ref, sgs_ref, o_ref, szm_ref, sgm_ref,
                 wglu_ref, bglu_ref, wbs_ref, wbm_ref, wout_ref, gpost_ref, out_ref):
    ge = _gelu_tanh(y_ref[0]).astype(_BF16)
    gl = jnp.dot(ge, wglu_ref[...], preferred_element_type=_F32) + bglu_ref[...]
    a = (gl[:, :SSM_WIDTH] * _sigmoid(gl[:, SSM_WIDTH:])) * szs_ref[0].astype(_F32)
    ys = jnp.dot(a.astype(_BF16), wbs_ref[...], preferred_element_type=_F32)
    am = o_ref[0].astype(_F32) * szm_ref[0].astype(_F32)
    ym = jnp.dot(am.astype(_BF16), wbm_ref[...], preferred_element_type=_F32)
    merged = sgs_ref[0].astype(_F32) * ys + sgm_ref[0].astype(_F32) * ym
    out = jnp.dot(merged.astype(_BF16), wout_ref[...], preferred_element_type=_F32)
    out_ref[0] = x_ref[0] + gate_ref[0] * (_rms(out) * gpost_ref[...])


def _post(x, gate, y, szs, sgs, o, szm, sgm, wglu, bglu, wbs, wbm, wout, gpost, tm):
    bsz, seq, _ = x.shape
    tok = pl.BlockSpec((1, tm, D_MODEL), lambda b, i: (b, i, 0))
    per_b = pl.BlockSpec((1, 1, D_MODEL), lambda b, i: (b, 0, 0))
    consts = [wglu, bglu, wbs, wbm, wout, gpost]
    return pl.pallas_call(
        _post_kernel,
        grid=(bsz, seq // tm),
        in_specs=[tok, per_b] + [tok] * 6 + [_const_spec(w.shape) for w in consts],
        out_specs=tok,
        out_shape=jax.ShapeDtypeStruct(x.shape, x.dtype),
        compiler_params=pltpu.CompilerParams(dimension_semantics=("parallel", "parallel"),
                                             vmem_limit_bytes=VMEM_LIMIT),
        name="post",
    )(x, gate, y, szs, sgs, o, szm, sgm, *consts)


def _mla_weights(w_q_up, w_kv_up):
    H = MLA_HEADS
    wq = w_q_up.reshape(Q_LORA, H, QK_NOPE + QK_ROPE)
    zq = jnp.zeros((Q_LORA, H, HEAD_PAD - QK_NOPE - QK_ROPE), w_q_up.dtype)
    wq_t = jnp.transpose(jnp.concatenate([wq, zq], axis=-1).reshape(Q_LORA, H * HEAD_PAD))
    wkv = w_kv_up.reshape(KV_LORA, H, QK_NOPE + V_HEAD)
    zk = jnp.zeros((KV_LORA, H, HEAD_PAD - QK_NOPE), w_kv_up.dtype)
    wk_main = jnp.concatenate([wkv[..., :QK_NOPE], zk], axis=-1).reshape(KV_LORA, H * HEAD_PAD)
    wv_t = jnp.transpose(wkv[..., QK_NOPE:].reshape(KV_LORA, H * V_HEAD))
    return [w.astype(_BF16) for w in (wq_t, wk_main, wv_t)]


def _rope_inv_freq_col():
    inv_freq = ROPE_BASE ** (-jnp.arange(0, QK_ROPE, 2, dtype=_F32) / QK_ROPE)
    return inv_freq.reshape(QK_ROPE // 2, 1)


def kernel(x, c, positions, w_ada, b_ada, g_pre, w_in, ssm_log_dt, ssm_a_re, ssm_a_im, ssm_b_re, ssm_b_im, ssm_c_re, ssm_c_im, ssm_d, w_glu, b_glu, g_q_norm, w_q_up, g_kv_norm, w_kv_up, w_br_ssm, w_br_mla, w_out, g_post):
    bsz, seq, _ = x.shape
    depth = w_ada.shape[0]
    assert bsz == SUBLANES and seq % ATT_TILE == 0
    tm = ATT_TILE
    sblk = min(512, seq)
    pos3 = positions.reshape(bsz, 1, seq)
    invf = _rope_inv_freq_col()
    off = np.cumsum((0, SSM_WIDTH, SSM_WIDTH, Q_LORA, KV_LORA, QK_ROPE, MLA_WIDTH, D_MODEL, D_MODEL))

    for l in range(depth):
        mod = _ada(c, w_ada[l], b_ada[l])
        shift, scale, gate = (mod[:, None, i * D_MODEL:(i + 1) * D_MODEL] for i in range(3))

        wb = w_in[l].astype(_BF16)
        seg = lambda i: wb[:, off[i]:off[i + 1]]
        w_lat = jnp.concatenate([seg(2), seg(3)], axis=1)
        w_krt = jnp.concatenate(
            [jnp.transpose(seg(4)), jnp.zeros((HEAD_PAD - QK_ROPE, D_MODEL), _BF16)], axis=0)
        ws = [seg(0), seg(1), seg(5), seg(6), seg(7), w_lat, w_krt,
              g_q_norm[l].reshape(1, Q_LORA), g_kv_norm[l].reshape(1, KV_LORA)]
        ws += _mla_weights(w_q_up[l], w_kv_up[l])
        u, szs, szm, sgs, sgm, qt, k, vt = _inproj(
            x, shift, scale, g_pre[l].reshape(1, D_MODEL), pos3, invf, ws, tm)

        w_t, w_s, w_g, a_r, a_i = _ssm_weights(
            ssm_log_dt[l], ssm_a_re[l], ssm_a_im[l], ssm_b_re[l], ssm_b_im[l],
            ssm_c_re[l], ssm_c_im[l], ssm_d[l])
        y = _ssm(u, w_t, w_s, w_g, a_r, a_i, sblk)

        o = _attn(qt, k, vt)

        x = _post(x, gate, y, szs, sgs, o, szm, sgm,
                  w_glu[l].astype(_BF16), b_glu[l].reshape(1, -1), w_br_ssm[l].astype(_BF16),
                  w_br_mla[l].astype(_BF16), w_out[l].astype(_BF16), g_post[l].reshape(1, D_MODEL),
                  min(POST_TILE, seq))
    return x
```

```python
import functools
import math

import jax
import jax.numpy as jnp
import numpy as np
from jax import lax
from jax.experimental import pallas as pl
from jax.experimental.pallas import tpu as pltpu

D_MODEL = 1024
CHUNK = 64
SSM_WIDTH = 1024
SSM_GROUP = 16
SSM_GROUPS = SSM_WIDTH // SSM_GROUP
SSM_STATE = 64
MLA_HEADS = 16
QK_NOPE = 64
QK_ROPE = 32
V_HEAD = 64
Q_LORA = 256
KV_LORA = 256
MLA_WIDTH = MLA_HEADS * V_HEAD
ROPE_BASE = 10000.0
EPS = 1e-6

LANES = 128
SUBLANES = 8
HEAD_PAD = 128
SSM_SUB = 8
SSM_BUNDLE = LANES // SSM_GROUP
SSM_NB = SSM_GROUPS // SSM_BUNDLE
SSM_HALF = SSM_BUNDLE // 2
SSM_NS = SSM_BUNDLE * SSM_STATE
SSM_BLOCK = 1024
ATT_TILE = 256
ATT_HEADS = 4
ATT_UNROLL = 8
ATT_BUFS = 2
POST_TILE = 512
ONES_ROWS = 16
VMEM_LIMIT = 56 * 1024 * 1024

_F32 = jnp.float32
_BF16 = jnp.bfloat16
_SOFTMAX_SCALE_LOG2E = ((QK_NOPE + QK_ROPE) ** -0.5) * math.log2(math.e)
_NEG_BIG = -1e30


def _const_spec(shape):
    zeros = (0,) * len(shape)
    return pl.BlockSpec(shape, lambda *_: zeros, pipeline_mode=pl.Buffered(1))


def _sigmoid(x):
    return 1.0 / (1.0 + jnp.exp(-x))


def _gelu_tanh(x):
    return 0.5 * x * (1.0 + jnp.tanh(math.sqrt(2.0 / math.pi) * (x + 0.044715 * (x * x * x))))


def _rms(x):
    return x * lax.rsqrt(jnp.mean(x * x, axis=-1, keepdims=True) + EPS)


def _ada_kernel(c_ref, w_ref, b_ref, o_ref):
    o_ref[...] = jnp.dot(c_ref[...], w_ref[...], preferred_element_type=_F32) + b_ref[...]


def _ada(c, w, b):
    bsz = c.shape[0]
    n = w.shape[1]
    bn = D_MODEL
    return pl.pallas_call(
        _ada_kernel,
        grid=(n // bn,),
        in_specs=[pl.BlockSpec((bsz, D_MODEL), lambda j: (0, 0)),
                  pl.BlockSpec((D_MODEL, bn), lambda j: (0, j)),
                  pl.BlockSpec((1, bn), lambda j: (0, j))],
        out_specs=pl.BlockSpec((bsz, bn), lambda j: (0, j)),
        out_shape=jax.ShapeDtypeStruct((bsz, n), _F32),
        compiler_params=pltpu.CompilerParams(dimension_semantics=("arbitrary",),
                                             vmem_limit_bytes=VMEM_LIMIT),
        name="ada",
    )(c, w, b.reshape(1, n))


def _inproj_kernel(x_ref, shift_ref, scale_ref, gpre_ref, pos_ref, invf_ref,
                   w_u_ref, w_zs_ref, w_zm_ref, w_gs_ref, w_gm_ref, w_lat_ref, w_krt_ref,
                   gq_ref, gkv_ref, wqt_ref, wk_ref, wvt_ref,
                   u_ref, szs_ref, szm_ref, sgs_ref, sgm_ref, qt_ref, k_ref, vt_ref):
    tm = x_ref.shape[1]
    half = QK_ROPE // 2
    r1 = slice(QK_NOPE, QK_NOPE + half)
    r2 = slice(QK_NOPE + half, QK_NOPE + QK_ROPE)
    nt = (((1,), (1,)), ((), ()))
    x = x_ref[0]
    h = _rms(x) * gpre_ref[...] * (1.0 + scale_ref[0]) + shift_ref[0]
    hb = h.astype(_BF16)

    def proj(w_ref):
        return jnp.dot(hb, w_ref[...], preferred_element_type=_F32)

    u_ref[0] = proj(w_u_ref)
    z = proj(w_zs_ref)
    szs_ref[0] = (z * _sigmoid(z)).astype(_BF16)
    z = proj(w_zm_ref)
    szm_ref[0] = (z * _sigmoid(z)).astype(_BF16)
    sgs_ref[0] = _sigmoid(proj(w_gs_ref)).astype(_BF16)
    sgm_ref[0] = _sigmoid(proj(w_gm_ref)).astype(_BF16)

    lat = proj(w_lat_ref)
    qn = (_rms(lat[:, :Q_LORA]) * gq_ref[...]).astype(_BF16)
    kvn = (_rms(lat[:, Q_LORA:]) * gkv_ref[...]).astype(_BF16)

    ang = invf_ref[...] * pos_ref[0].astype(_F32)
    cos = jnp.cos(ang)
    sin = jnp.sin(ang)

    qt = lax.dot_general(wqt_ref[...], qn, nt, preferred_element_type=_F32)
    for hd in range(MLA_HEADS):
        blk = qt[hd * HEAD_PAD:(hd + 1) * HEAD_PAD]
        t1, t2 = blk[r1], blk[r2]
        blk = jnp.concatenate([blk[:QK_NOPE], t1 * cos - t2 * sin, t2 * cos + t1 * sin,
                               blk[QK_NOPE + QK_ROPE:]], axis=0)
        qt_ref[0, 0, hd * HEAD_PAD:(hd + 1) * HEAD_PAD, :] = (blk * _SOFTMAX_SCALE_LOG2E).astype(_BF16)

    krt = lax.dot_general(w_krt_ref[...], hb, nt, preferred_element_type=_F32)
    t1, t2 = krt[:half], krt[half:QK_ROPE]
    spare = HEAD_PAD - QK_NOPE - QK_ROPE
    row = lax.broadcasted_iota(jnp.int32, (spare, tm), 0)
    col_chunk = lax.broadcasted_iota(jnp.int32, (spare, tm), 1) // CHUNK
    onehot = jnp.where(row == col_chunk, 1.0, 0.0)
    kr_full = jnp.concatenate([jnp.zeros((QK_NOPE, tm), _F32), t1 * cos - t2 * sin,
                               t2 * cos + t1 * sin, onehot], axis=0)
    kr_place = jnp.transpose(kr_full)
    ka = jnp.dot(kvn, wk_ref[...], preferred_element_type=_F32)
    for hd in range(MLA_HEADS):
        k_ref[0, hd] = (ka[:, hd * HEAD_PAD:(hd + 1) * HEAD_PAD] + kr_place).astype(_BF16)
    vt_ref[0, 0] = lax.dot_general(wvt_ref[...], kvn, nt, preferred_element_type=_F32).astype(_BF16)


def _inproj(x, shift, scale, g_pre, pos3, invf, ws, tm):
    bsz, seq, _ = x.shape
    tok = lambda w: pl.BlockSpec((1, tm, w), lambda b, i: (b, i, 0))
    per_b = pl.BlockSpec((1, 1, D_MODEL), lambda b, i: (b, 0, 0))
    head = pl.BlockSpec((1, MLA_HEADS, tm, HEAD_PAD), lambda b, i: (b, 0, i, 0))
    tiled = lambda rows: pl.BlockSpec((1, 1, rows, tm), lambda b, i: (b, i, 0, 0))
    tshape = lambda rows: jax.ShapeDtypeStruct((bsz, seq // tm, rows, tm), _BF16)
    act = jax.ShapeDtypeStruct((bsz, seq, D_MODEL), _BF16)
    w_specs = [_const_spec(w.shape) for w in ws]
    return pl.pallas_call(
        _inproj_kernel,
        grid=(bsz, seq // tm),
        in_specs=[tok(D_MODEL), per_b, per_b, _const_spec((1, D_MODEL)),
                  pl.BlockSpec((1, 1, tm), lambda b, i: (b, 0, i)),
                  _const_spec((QK_ROPE // 2, 1))] + w_specs,
        out_specs=[tok(D_MODEL)] * 5 + [tiled(MLA_HEADS * HEAD_PAD), head, tiled(MLA_WIDTH)],
        out_shape=[jax.ShapeDtypeStruct((bsz, seq, SSM_WIDTH), _F32)] + [act] * 4
                  + [tshape(MLA_HEADS * HEAD_PAD),
                     jax.ShapeDtypeStruct((bsz, MLA_HEADS, seq, HEAD_PAD), _BF16), tshape(MLA_WIDTH)],
        compiler_params=pltpu.CompilerParams(dimension_semantics=("parallel", "parallel"),
                                             vmem_limit_bytes=VMEM_LIMIT),
        name="inproj",
    )(x, shift, scale, g_pre, pos3, invf, *ws)


def _ssm_weights(log_dt, a_re, a_im, b_re, b_im, c_re, c_im, d_skip):
    hi = lax.Precision.HIGHEST
    L, G, N, P = SSM_SUB, SSM_GROUPS, SSM_STATE, SSM_GROUP
    nb, gb = G // SSM_HALF, SSM_HALF
    flat = L * gb * P
    dt = jnp.exp(log_dt.astype(_F32))[:, None]
    lr, li = a_re.astype(_F32), a_im.astype(_F32)
    mag = jnp.exp(lr * dt)
    abar_re, abar_im = mag * jnp.cos(li * dt), mag * jnp.sin(li * dt)
    den = lr * lr + li * li
    nr, ni = abar_re - 1.0, abar_im
    fr = (nr * lr + ni * li) / den
    fi = (ni * lr - nr * li) / den
    br, bi = b_re.astype(_F32), b_im.astype(_F32)
    bbr = fr[..., None] * br - fi[..., None] * bi
    bbi = fr[..., None] * bi + fi[..., None] * br
    cr, ci = c_re.astype(_F32), c_im.astype(_F32)
    j = jnp.arange(L + 1, dtype=_F32)[:, None, None]
    pmag = jnp.exp(lr * dt * j)
    pr, pi_ = pmag * jnp.cos(li * dt * j), pmag * jnp.sin(li * dt * j)
    abr = pr[:L, :, :, None] * bbr - pi_[:L, :, :, None] * bbi
    abi = pr[:L, :, :, None] * bbi + pi_[:L, :, :, None] * bbr
    kj = (jnp.einsum('gpn,jgnq->jgpq', cr, abr, precision=hi)
          - jnp.einsum('gpn,jgnq->jgpq', ci, abi, precision=hi))
    kj = kj.at[0].add(jnp.eye(P, dtype=_F32) * d_skip.astype(_F32).reshape(G, P)[:, :, None])
    lag = jnp.arange(L)[None, :] - jnp.arange(L)[:, None]
    toe = jnp.where((lag >= 0)[:, :, None, None, None], kj[jnp.clip(lag, 0)], 0.0)
    toe = toe.reshape(L, L, nb, gb, P, P)
    t_c = jnp.transpose(toe, (2, 0, 3, 5, 1, 4)).reshape(nb, flat, L * P)
    ab = jnp.stack([abr[::-1], abi[::-1]], axis=0).reshape(2, L, nb, gb, N, P)
    s_c = jnp.transpose(ab, (2, 1, 3, 5, 0, 4)).reshape(nb, flat, 2 * N)
    er = cr[None] * pr[1:, :, None, :] - ci[None] * pi_[1:, :, None, :]
    ei = cr[None] * pi_[1:, :, None, :] + ci[None] * pr[1:, :, None, :]
    e = jnp.stack([er, -ei], axis=0).reshape(2, L, nb, gb, P, N)
    g_c = jnp.transpose(e, (2, 0, 3, 5, 1, 4)).reshape(nb, 2 * gb * N, L * P)

    def spread(compact, col_block, row_block):
        width = compact.shape[-1] * gb
        lane = np.arange(width)
        src = (lane // (gb * col_block)) * col_block + lane % col_block
        onehot = jnp.asarray(np.arange(compact.shape[-1])[:, None] == src[None, :], _BF16)
        full = jnp.einsum('brk,kc->brc', compact.astype(_BF16), onehot, preferred_element_type=_F32)
        row_g = (np.arange(compact.shape[1]) // row_block) % gb
        col_g = (lane // col_block) % gb
        return jnp.where(jnp.asarray(row_g[:, None] == col_g[None, :]), full, 0.0).astype(_BF16)

    w_t = spread(t_c, P, P)
    w_s = spread(s_c, N, P)
    w_g = spread(g_c, P, N)
    a_r = pr[L].reshape(SSM_NB, 1, SSM_NS)
    a_i = pi_[L].reshape(SSM_NB, 1, SSM_NS)
    return w_t, w_s, w_g, a_r, a_i


def _ssm_kernel(u_ref, wt_ref, ws_ref, wg_ref, ar_ref, ai_ref, y_ref, lhs_scr, v_scr, yp_scr, st_scr):
    bsz, sblk, _ = u_ref.shape
    nsub = sblk // SSM_SUB
    ns = SSM_NS
    hl = SSM_HALF * SSM_GROUP
    hs = SSM_HALF * SSM_STATE
    hf = SSM_SUB * hl

    @pl.when(pl.program_id(1) == 0)
    def _():
        st_scr[...] = jnp.zeros_like(st_scr)

    for b in range(bsz):
        for s in range(SSM_SUB):
            lhs_scr[s, pl.ds(b, nsub, stride=bsz), :] = u_ref[b, pl.ds(s, nsub, stride=SSM_SUB), :]
    slabs = [lhs_scr[s] for s in range(SSM_SUB)]
    lhs = [jnp.concatenate([x[:, h * hl:(h + 1) * hl] for x in slabs], axis=-1).astype(_BF16)
           for h in range(2)]
    for h in range(2):
        v = jnp.dot(lhs[h], ws_ref[h], preferred_element_type=_F32)
        v_scr[:, h * hs:(h + 1) * hs] = v[:, :hs]
        v_scr[:, ns + h * hs:ns + (h + 1) * hs] = v[:, hs:]

    ar = jnp.broadcast_to(ar_ref[0], (bsz, ns))
    ai = jnp.broadcast_to(ai_ref[0], (bsz, ns))

    def step(c, carry):
        sr, si = carry
        r0 = pl.multiple_of(c * bsz, bsz)
        vr = v_scr[pl.ds(r0, bsz), 0:ns]
        vi = v_scr[pl.ds(r0, bsz), ns:2 * ns]
        v_scr[pl.ds(r0, bsz), 0:ns] = sr
        v_scr[pl.ds(r0, bsz), ns:2 * ns] = si
        return ar * sr - ai * si + vr, ar * si + ai * sr + vi

    sr, si = lax.fori_loop(0, nsub, step, (st_scr[0], st_scr[1]), unroll=4)
    st_scr[0] = sr
    st_scr[1] = si

    half_f = hf // 2
    yp = []
    for h in range(2):
        s_prev = jnp.concatenate([v_scr[:, h * hs:(h + 1) * hs],
                                  v_scr[:, ns + h * hs:ns + (h + 1) * hs]], axis=-1).astype(_BF16)
        carry_in = jnp.dot(s_prev, wg_ref[h], preferred_element_type=_F32)
        lo = carry_in[:, :half_f] + jnp.dot(lhs[h][:, :half_f], wt_ref[h, :half_f, :half_f],
                                            preferred_element_type=_F32)
        hi = carry_in[:, half_f:] + jnp.dot(lhs[h], wt_ref[h, :, half_f:], preferred_element_type=_F32)
        yp.append(jnp.concatenate([lo, hi], axis=-1))
    for t in range(SSM_SUB):
        yp_scr[t] = jnp.concatenate([yp[h][:, t * hl:(t + 1) * hl] for h in range(2)], axis=-1)
    for b in range(bsz):
        for t in range(SSM_SUB):
            y_ref[b, pl.ds(t, nsub, stride=SSM_SUB), :] = yp_scr[t, pl.ds(b, nsub, stride=bsz), :]


def _ssm(u, w_t, w_s, w_g, a_r, a_i, sblk):
    bsz, seq, _ = u.shape
    rows = (sblk // SSM_SUB) * bsz
    hf = SSM_SUB * SSM_HALF * SSM_GROUP
    tok = pl.BlockSpec((bsz, sblk, LANES), lambda g, i: (0, i, g))
    wspec = pl.BlockSpec((2, hf, hf), lambda g, i: (g, 0, 0))
    aspec = pl.BlockSpec((1, 1, SSM_NS), lambda g, i: (g, 0, 0))
    return pl.pallas_call(
        _ssm_kernel,
        grid=(SSM_NB, seq // sblk),
        in_specs=[tok, wspec, wspec, wspec, aspec, aspec],
        out_specs=tok,
        out_shape=jax.ShapeDtypeStruct(u.shape, _F32),
        scratch_shapes=[pltpu.VMEM((SSM_SUB, rows, LANES), _F32),
                        pltpu.VMEM((rows, 2 * SSM_NS), _F32),
                        pltpu.VMEM((SSM_SUB, rows, LANES), _F32),
                        pltpu.VMEM((2, bsz, SSM_NS), _F32)],
        compiler_params=pltpu.CompilerParams(dimension_semantics=("parallel", "arbitrary"),
                                             vmem_limit_bytes=VMEM_LIMIT),
        name="ssm",
    )(u, w_t, w_s, w_g, a_r, a_i)


def _attn_kernel(qt_ref, k_ref, vt_ref, o_ref, s_ref, acc_ref, pen_ref):
    t = ATT_TILE
    nq = qt_ref.shape[1]
    nh = k_ref.shape[1]
    ones = jnp.ones((ONES_ROWS, t), _BF16)

    row = lax.broadcasted_iota(jnp.int32, (HEAD_PAD, t), 0) - (QK_NOPE + QK_ROPE)
    qry_chunk = lax.broadcasted_iota(jnp.int32, (HEAD_PAD, t), 1) // CHUNK
    pen_ref[0] = jnp.zeros((HEAD_PAD, t), _BF16)
    pen_ref[1] = jnp.where((row >= 0) & (row < t // CHUNK) & (row > qry_chunk), _NEG_BIG, 0.0).astype(_BF16)

    def scores(dst_ref, qi, kj):
        k0 = pl.multiple_of(kj * t, t)
        pen = pen_ref[(kj == qi).astype(jnp.int32)]
        tile_max = []
        for hh in range(nh):
            qt = qt_ref[0, qi, hh * HEAD_PAD:(hh + 1) * HEAD_PAD, :] + pen
            st = jnp.dot(k_ref[0, hh, pl.ds(k0, t), :], qt, preferred_element_type=_F32)
            dst_ref[hh] = st
            tile_max.append(jnp.max(st, axis=0, keepdims=True))
        return tuple(tile_max)

    def step(cur_ref, nxt_ref, carry):
        qi, kj, tmax, ms, accs = carry
        last = kj == qi
        kj_n = jnp.where(last, 0, kj + 1)
        qi_n = jnp.where(last, qi + 1, qi)
        tmax_n = scores(nxt_ref, jnp.minimum(qi_n, nq - 1), kj_n)
        ms_n, accs_n = [], []
        for hh in range(nh):
            m_new = jnp.maximum(ms[hh], tmax[hh])
            alpha = jnp.exp2(ms[hh] - m_new)
            p = jnp.exp2(cur_ref[hh] - m_new).astype(_BF16)
            vte = jnp.concatenate([vt_ref[0, kj, hh * V_HEAD:(hh + 1) * V_HEAD, :], ones], axis=0)
            acc = alpha * accs[hh] + jnp.dot(vte, p, preferred_element_type=_F32)
            acc_ref[qi, hh] = acc
            ms_n.append(jnp.where(last, _NEG_BIG, m_new))
            accs_n.append(jnp.where(last, 0.0, acc))
        return qi_n, kj_n, tmax_n, tuple(ms_n), tuple(accs_n)

    zero = jnp.int32(0)
    bufs = [s_ref.at[i] for i in range(ATT_BUFS)]
    init = (zero, zero, scores(bufs[0], zero, zero),
            tuple(jnp.full((1, t), _NEG_BIG, _F32) for _ in range(nh)),
            tuple(jnp.zeros((V_HEAD + ONES_ROWS, t), _F32) for _ in range(nh)))
    nsteps = nq * (nq + 1) // 2

    def run(count, c):
        for i in range(count):
            c = step(bufs[i % ATT_BUFS], bufs[(i + 1) % ATT_BUFS], c)
        return c

    assert ATT_UNROLL % ATT_BUFS == 0
    carry = lax.fori_loop(0, nsteps // ATT_UNROLL, lambda _, c: run(ATT_UNROLL, c), init)
    run(nsteps % ATT_UNROLL, carry)

    def finish(qi, _):
        outs = [acc_ref[qi, hh, :V_HEAD] / acc_ref[qi, hh, V_HEAD:V_HEAD + 1] for hh in range(nh)]
        o_ref[0, pl.ds(pl.multiple_of(qi * t, t), t), :] = (
            jnp.transpose(jnp.concatenate(outs, axis=0)).astype(_BF16))
        return 0

    lax.fori_loop(0, nq, finish, 0)


def _attn(qt, k, vt):
    bsz, heads, seq, _ = k.shape
    nh = ATT_HEADS
    k_spec = pl.BlockSpec((1, nh, seq, HEAD_PAD), lambda b, h: (b, h, 0, 0))
    tiled = lambda rows: pl.BlockSpec((1, seq // ATT_TILE, nh * rows, ATT_TILE), lambda b, h: (b, 0, h, 0))
    return pl.pallas_call(
        _attn_kernel,
        grid=(bsz, heads // nh),
        in_specs=[tiled(HEAD_PAD), k_spec, tiled(V_HEAD)],
        out_specs=pl.BlockSpec((1, seq, nh * V_HEAD), lambda b, h: (b, 0, h)),
        out_shape=jax.ShapeDtypeStruct((bsz, seq, MLA_WIDTH), _BF16),
        scratch_shapes=[pltpu.VMEM((ATT_BUFS, nh, ATT_TILE, ATT_TILE), _F32),
                        pltpu.VMEM((seq // ATT_TILE, nh, V_HEAD + ONES_ROWS, ATT_TILE), _F32),
                        pltpu.VMEM((2, HEAD_PAD, ATT_TILE), _BF16)],
        compiler_params=pltpu.CompilerParams(dimension_semantics=("parallel", "parallel"),
                                             vmem_limit_bytes=VMEM_LIMIT),
        name="attn",
    )(qt, k, vt)


def _post_kernel(x_ref, gate_ref, y_ref, szs_ref, sgs_ref, o_ref, szm_ref, sgm_ref,
                 wglu_ref, bglu_ref, wbs_ref, wbm_ref, wout_ref, gpost_ref, out_ref):
    ge = _gelu_tanh(y_ref[0]).astype(_BF16)
    gl = jnp.dot(ge, wglu_ref[...], preferred_element_type=_F32) + bglu_ref[...]
    a = (gl[:, :SSM_WIDTH] * _sigmoid(gl[:, SSM_WIDTH:])) * szs_ref[0].astype(_F32)
    ys = jnp.dot(a.astype(_BF16), wbs_ref[...], preferred_element_type=_F32)
    am = o_ref[0].astype(_F32) * szm_ref[0].astype(_F32)
    ym = jnp.dot(am.astype(_BF16), wbm_ref[...], preferred_element_type=_F32)
    merged = sgs_ref[0].astype(_F32) * ys + sgm_ref[0].astype(_F32) * ym
    out = jnp.dot(merged.astype(_BF16), wout_ref[...], preferred_element_type=_F32)
    out_ref[0] = x_ref[0] + gate_ref[0] * (_rms(out) * gpost_ref[...])


def _post(x, gate, y, szs, sgs, o, szm, sgm, wglu, bglu, wbs, wbm, wout, gpost, tm):
    bsz, seq, _ = x.shape
    tok = pl.BlockSpec((1, tm, D_MODEL), lambda b, i: (b, i, 0))
    per_b = pl.BlockSpec((1, 1, D_MODEL), lambda b, i: (b, 0, 0))
    consts = [wglu, bglu, wbs, wbm, wout, gpost]
    return pl.pallas_call(
        _post_kernel,
        grid=(bsz, seq // tm),
        in_specs=[tok, per_b] + [tok] * 6 + [_const_spec(w.shape) for w in consts],
        out_specs=tok,
        out_shape=jax.ShapeDtypeStruct(x.shape, x.dtype),
        compiler_params=pltpu.CompilerParams(dimension_semantics=("parallel", "parallel"),
                                             vmem_limit_bytes=VMEM_LIMIT),
        name="post",
    )(x, gate, y, szs, sgs, o, szm, sgm, *consts)


def _mla_weights(w_q_up, w_kv_up):
    H = MLA_HEADS
    wq = w_q_up.reshape(Q_LORA, H, QK_NOPE + QK_ROPE)
    zq = jnp.zeros((Q_LORA, H, HEAD_PAD - QK_NOPE - QK_ROPE), w_q_up.dtype)
    wq_t = jnp.transpose(jnp.concatenate([wq, zq], axis=-1).reshape(Q_LORA, H * HEAD_PAD))
    wkv = w_kv_up.reshape(KV_LORA, H, QK_NOPE + V_HEAD)
    zk = jnp.zeros((KV_LORA, H, HEAD_PAD - QK_NOPE), w_kv_up.dtype)
    wk_main = jnp.concatenate([wkv[..., :QK_NOPE], zk], axis=-1).reshape(KV_LORA, H * HEAD_PAD)
    wv_t = jnp.transpose(wkv[..., QK_NOPE:].reshape(KV_LORA, H * V_HEAD))
    return [w.astype(_BF16) for w in (wq_t, wk_main, wv_t)]


def _rope_inv_freq_col():
    inv_freq = ROPE_BASE ** (-jnp.arange(0, QK_ROPE, 2, dtype=_F32) / QK_ROPE)
    return inv_freq.reshape(QK_ROPE // 2, 1)


def kernel(x, c, positions, w_ada, b_ada, g_pre, w_in, ssm_log_dt, ssm_a_re, ssm_a_im, ssm_b_re, ssm_b_im, ssm_c_re, ssm_c_im, ssm_d, w_glu, b_glu, g_q_norm, w_q_up, g_kv_norm, w_kv_up, w_br_ssm, w_br_mla, w_out, g_post):
    bsz, seq, _ = x.shape
    depth = w_ada.shape[0]
    assert bsz == SUBLANES and seq % ATT_TILE == 0
    tm = ATT_TILE
    sblk = min(SSM_BLOCK, seq)
    pos3 = positions.reshape(bsz, 1, seq)
    invf = _rope_inv_freq_col()
    off = np.cumsum((0, SSM_WIDTH, SSM_WIDTH, Q_LORA, KV_LORA, QK_ROPE, MLA_WIDTH, D_MODEL, D_MODEL))

    for l in range(depth):
        mod = _ada(c, w_ada[l], b_ada[l])
        shift, scale, gate = (mod[:, None, i * D_MODEL:(i + 1) * D_MODEL] for i in range(3))

        seg = lambda i, j=None: w_in[l][:, off[i]:off[(i if j is None else j) + 1]].astype(_BF16)
        w_lat = seg(2, 3)
        w_krt = jnp.concatenate(
            [jnp.transpose(seg(4)), jnp.zeros((HEAD_PAD - QK_ROPE, D_MODEL), _BF16)], axis=0)
        ws = [seg(0), seg(1), seg(5), seg(6), seg(7), w_lat, w_krt,
              g_q_norm[l].reshape(1, Q_LORA), g_kv_norm[l].reshape(1, KV_LORA)]
        ws += _mla_weights(w_q_up[l], w_kv_up[l])
        u, szs, szm, sgs, sgm, qt, k, vt = _inproj(
            x, shift, scale, g_pre[l].reshape(1, D_MODEL), pos3, invf, ws, tm)

        w_t, w_s, w_g, a_r, a_i = _ssm_weights(
            ssm_log_dt[l], ssm_a_re[l], ssm_a_im[l], ssm_b_re[l], ssm_b_im[l],
            ssm_c_re[l], ssm_c_im[l], ssm_d[l])
        y = _ssm(u, w_t, w_s, w_g, a_r, a_i, sblk)

        o = _attn(qt, k, vt)

        x = _post(x, gate, y, szs, sgs, o, szm, sgm,
                  w_glu[l].astype(_BF16), b_glu[l].reshape(1, -1), w_br_ssm[l].astype(_BF16),
                  w_br_mla[l].astype(_BF16), w_out[l].astype(_BF16), g_post[l].reshape(1, D_MODEL),
                  min(POST_TILE, seq))
    return x
```

```python
import functools
import math

import jax
import jax.numpy as jnp
import numpy as np
from jax import lax
from jax.experimental import pallas as pl
from jax.experimental.pallas import tpu as pltpu

D_MODEL = 1024
CHUNK = 64
SSM_WIDTH = 1024
SSM_GROUP = 16
SSM_GROUPS = SSM_WIDTH // SSM_GROUP
SSM_STATE = 64
MLA_HEADS = 16
QK_NOPE = 64
QK_ROPE = 32
V_HEAD = 64
Q_LORA = 256
KV_LORA = 256
MLA_WIDTH = MLA_HEADS * V_HEAD
ROPE_BASE = 10000.0
EPS = 1e-6

LANES = 128
SUBLANES = 8
HEAD_PAD = 128
SSM_SUB = 8
SSM_BUNDLE = LANES // SSM_GROUP
SSM_NB = SSM_GROUPS // SSM_BUNDLE
SSM_HALF = SSM_BUNDLE // 2
SSM_NS = SSM_BUNDLE * SSM_STATE
SSM_BLOCK = 1024
ATT_TILE = 256
ATT_HEADS = 4
ATT_UNROLL = 8
ATT_BUFS = 2
POST_TILE = 512
ONES_ROWS = 16
VMEM_LIMIT = 56 * 1024 * 1024

_F32 = jnp.float32
_BF16 = jnp.bfloat16
_SOFTMAX_SCALE_LOG2E = ((QK_NOPE + QK_ROPE) ** -0.5) * math.log2(math.e)
_NEG_BIG = -1e30


def _const_spec(shape):
    zeros = (0,) * len(shape)
    return pl.BlockSpec(shape, lambda *_: zeros, pipeline_mode=pl.Buffered(1))


def _sigmoid(x):
    return 1.0 / (1.0 + jnp.exp(-x))


def _gelu_tanh(x):
    return 0.5 * x * (1.0 + jnp.tanh(math.sqrt(2.0 / math.pi) * (x + 0.044715 * (x * x * x))))


def _rms(x):
    return x * lax.rsqrt(jnp.mean(x * x, axis=-1, keepdims=True) + EPS)


def _ada_kernel(c_ref, w_ref, b_ref, o_ref):
    o_ref[...] = jnp.dot(c_ref[...], w_ref[...], preferred_element_type=_F32) + b_ref[...]


def _ada(c, w, b):
    bsz = c.shape[0]
    n = w.shape[1]
    bn = D_MODEL
    return pl.pallas_call(
        _ada_kernel,
        grid=(n // bn,),
        in_specs=[pl.BlockSpec((bsz, D_MODEL), lambda j: (0, 0)),
                  pl.BlockSpec((D_MODEL, bn), lambda j: (0, j)),
                  pl.BlockSpec((1, bn), lambda j: (0, j))],
        out_specs=pl.BlockSpec((bsz, bn), lambda j: (0, j)),
        out_shape=jax.ShapeDtypeStruct((bsz, n), _F32),
        compiler_params=pltpu.CompilerParams(dimension_semantics=("arbitrary",),
                                             vmem_limit_bytes=VMEM_LIMIT),
        name="ada",
    )(c, w, b.reshape(1, n))


def _inproj_kernel(x_ref, shift_ref, scale_ref, gpre_ref, pos_ref, invf_ref,
                   w_u_ref, w_zs_ref, w_zm_ref, w_gs_ref, w_gm_ref, w_lat_ref, w_krt_ref,
                   gq_ref, gkv_ref, wqt_ref, wk_ref, wvt_ref,
                   u_ref, szs_ref, szm_ref, sgs_ref, sgm_ref, qt_ref, k_ref, vt_ref):
    tm = x_ref.shape[1]
    half = QK_ROPE // 2
    r1 = slice(QK_NOPE, QK_NOPE + half)
    r2 = slice(QK_NOPE + half, QK_NOPE + QK_ROPE)
    nt = (((1,), (1,)), ((), ()))
    x = x_ref[0]
    h = _rms(x) * gpre_ref[...] * (1.0 + scale_ref[0]) + shift_ref[0]
    hb = h.astype(_BF16)

    def proj(w_ref):
        return jnp.dot(hb, w_ref[...], preferred_element_type=_F32)

    u_ref[0] = proj(w_u_ref)
    z = proj(w_zs_ref)
    szs_ref[0] = (z * _sigmoid(z)).astype(_BF16)
    z = proj(w_zm_ref)
    szm_ref[0] = (z * _sigmoid(z)).astype(_BF16)
    sgs_ref[0] = _sigmoid(proj(w_gs_ref)).astype(_BF16)
    sgm_ref[0] = _sigmoid(proj(w_gm_ref)).astype(_BF16)

    lat = proj(w_lat_ref)
    qn = (_rms(lat[:, :Q_LORA]) * gq_ref[...]).astype(_BF16)
    kvn = (_rms(lat[:, Q_LORA:]) * gkv_ref[...]).astype(_BF16)

    ang = invf_ref[...] * pos_ref[0].astype(_F32)
    cos = jnp.cos(ang)
    sin = jnp.sin(ang)

    qt = lax.dot_general(wqt_ref[...], qn, nt, preferred_element_type=_F32)
    for hd in range(MLA_HEADS):
        blk = qt[hd * HEAD_PAD:(hd + 1) * HEAD_PAD]
        t1, t2 = blk[r1], blk[r2]
        blk = jnp.concatenate([blk[:QK_NOPE], t1 * cos - t2 * sin, t2 * cos + t1 * sin,
                               blk[QK_NOPE + QK_ROPE:]], axis=0)
        qt_ref[0, 0, hd * HEAD_PAD:(hd + 1) * HEAD_PAD, :] = (blk * _SOFTMAX_SCALE_LOG2E).astype(_BF16)

    krt = lax.dot_general(w_krt_ref[...], hb, nt, preferred_element_type=_F32)
    t1, t2 = krt[:half], krt[half:QK_ROPE]
    spare = HEAD_PAD - QK_NOPE - QK_ROPE
    row = lax.broadcasted_iota(jnp.int32, (spare, tm), 0)
    col_chunk = lax.broadcasted_iota(jnp.int32, (spare, tm), 1) // CHUNK
    onehot = jnp.where(row == col_chunk, 1.0, 0.0)
    kr_full = jnp.concatenate([jnp.zeros((QK_NOPE, tm), _F32), t1 * cos - t2 * sin,
                               t2 * cos + t1 * sin, onehot], axis=0)
    kr_place = jnp.transpose(kr_full)
    ka = jnp.dot(kvn, wk_ref[...], preferred_element_type=_F32)
    for hd in range(MLA_HEADS):
        k_ref[0, hd] = (ka[:, hd * HEAD_PAD:(hd + 1) * HEAD_PAD] + kr_place).astype(_BF16)
    vt_ref[0, 0] = lax.dot_general(wvt_ref[...], kvn, nt, preferred_element_type=_F32).astype(_BF16)


def _inproj(x, shift, scale, g_pre, pos3, invf, ws, tm):
    bsz, seq, _ = x.shape
    tok = lambda w: pl.BlockSpec((1, tm, w), lambda b, i: (b, i, 0))
    per_b = pl.BlockSpec((1, 1, D_MODEL), lambda b, i: (b, 0, 0))
    head = pl.BlockSpec((1, MLA_HEADS, tm, HEAD_PAD), lambda b, i: (b, 0, i, 0))
    tiled = lambda rows: pl.BlockSpec((1, 1, rows, tm), lambda b, i: (b, i, 0, 0))
    tshape = lambda rows: jax.ShapeDtypeStruct((bsz, seq // tm, rows, tm), _BF16)
    act = jax.ShapeDtypeStruct((bsz, seq, D_MODEL), _BF16)
    w_specs = [_const_spec(w.shape) for w in ws]
    return pl.pallas_call(
        _inproj_kernel,
        grid=(bsz, seq // tm),
        in_specs=[tok(D_MODEL), per_b, per_b, _const_spec((1, D_MODEL)),
                  pl.BlockSpec((1, 1, tm), lambda b, i: (b, 0, i)),
                  _const_spec((QK_ROPE // 2, 1))] + w_specs,
        out_specs=[tok(D_MODEL)] * 5 + [tiled(MLA_HEADS * HEAD_PAD), head, tiled(MLA_WIDTH)],
        out_shape=[jax.ShapeDtypeStruct((bsz, seq, SSM_WIDTH), _F32)] + [act] * 4
                  + [tshape(MLA_HEADS * HEAD_PAD),
                     jax.ShapeDtypeStruct((bsz, MLA_HEADS, seq, HEAD_PAD), _BF16), tshape(MLA_WIDTH)],
        compiler_params=pltpu.CompilerParams(dimension_semantics=("parallel", "parallel"),
                                             vmem_limit_bytes=VMEM_LIMIT),
        name="inproj",
    )(x, shift, scale, g_pre, pos3, invf, *ws)


def _ssm_weights(log_dt, a_re, a_im, b_re, b_im, c_re, c_im, d_skip):
    hi = lax.Precision.HIGHEST
    L, G, N, P = SSM_SUB, SSM_GROUPS, SSM_STATE, SSM_GROUP
    nb, gb = G // SSM_HALF, SSM_HALF
    dt = jnp.exp(log_dt.astype(_F32))[:, None]
    lr, li = a_re.astype(_F32), a_im.astype(_F32)
    mag = jnp.exp(lr * dt)
    abar_re, abar_im = mag * jnp.cos(li * dt), mag * jnp.sin(li * dt)
    den = lr * lr + li * li
    nr, ni = abar_re - 1.0, abar_im
    fr = (nr * lr + ni * li) / den
    fi = (ni * lr - nr * li) / den
    br, bi = b_re.astype(_F32), b_im.astype(_F32)
    bbr = fr[..., None] * br - fi[..., None] * bi
    bbi = fr[..., None] * bi + fi[..., None] * br
    cr, ci = c_re.astype(_F32), c_im.astype(_F32)
    j = jnp.arange(L + 1, dtype=_F32)[:, None, None]
    pmag = jnp.exp(lr * dt * j)
    pr, pi_ = pmag * jnp.cos(li * dt * j), pmag * jnp.sin(li * dt * j)
    bbr_t, bbi_t = jnp.swapaxes(bbr, 1, 2), jnp.swapaxes(bbi, 1, 2)
    abr = pr[:L, :, None, :] * bbr_t - pi_[:L, :, None, :] * bbi_t
    abi = pr[:L, :, None, :] * bbi_t + pi_[:L, :, None, :] * bbr_t
    kj = (jnp.einsum('gpn,jgqn->gqjp', cr, abr, precision=hi)
          - jnp.einsum('gpn,jgqn->gqjp', ci, abi, precision=hi))
    kj = kj.at[:, :, 0, :].add(jnp.eye(P, dtype=_F32)[None] * d_skip.astype(_F32).reshape(G, 1, P))
    kj = kj.reshape(G, P, L * P)

    def by_half_slab(x):
        lead, _, rows, width = x.shape
        x = x.reshape(lead, nb, gb, rows, width)
        return jnp.transpose(x, (1, 0, 2, 3, 4)).reshape(nb, lead * gb * rows, width)

    t_c = by_half_slab(jnp.stack(
        [jnp.pad(kj, ((0, 0), (0, 0), (s * P, 0)))[:, :, :L * P] for s in range(L)], axis=0))
    s_c = by_half_slab(jnp.concatenate([abr[::-1], abi[::-1]], axis=-1))
    c_rep = lambda c: jnp.tile(jnp.swapaxes(c, 1, 2), (1, 1, L))
    p_rep = lambda a: jnp.repeat(jnp.transpose(a[1:], (1, 2, 0)), P, axis=-1)
    crr, cir, prr, pir = c_rep(cr), c_rep(ci), p_rep(pr), p_rep(pi_)
    g_c = by_half_slab(jnp.stack([crr * prr - cir * pir, -(crr * pir + cir * prr)], axis=0))

    def spread(compact, col_block, row_block):
        width = compact.shape[-1] * gb
        lane = np.arange(width)
        src = (lane // (gb * col_block)) * col_block + lane % col_block
        onehot = jnp.asarray(np.arange(compact.shape[-1])[:, None] == src[None, :], _BF16)
        full = jnp.einsum('brk,kc->brc', compact.astype(_BF16), onehot, preferred_element_type=_F32)
        row_g = (np.arange(compact.shape[1]) // row_block) % gb
        col_g = (lane // col_block) % gb
        return jnp.where(jnp.asarray(row_g[:, None] == col_g[None, :]), full, 0.0).astype(_BF16)

    w_t = spread(t_c, P, P)
    w_s = spread(s_c, N, P)
    w_g = spread(g_c, P, N)
    a_r = pr[L].reshape(SSM_NB, 1, SSM_NS)
    a_i = pi_[L].reshape(SSM_NB, 1, SSM_NS)
    return w_t, w_s, w_g, a_r, a_i


def _ssm_kernel(u_ref, wt_ref, ws_ref, wg_ref, ar_ref, ai_ref, y_ref, lhs_scr, v_scr, yp_scr, st_scr):
    bsz, sblk, _ = u_ref.shape
    nsub = sblk // SSM_SUB
    ns = SSM_NS
    hl = SSM_HALF * SSM_GROUP
    hs = SSM_HALF * SSM_STATE
    hf = SSM_SUB * hl

    @pl.when(pl.program_id(1) == 0)
    def _():
        st_scr[...] = jnp.zeros_like(st_scr)

    for b in range(bsz):
        for s in range(SSM_SUB):
            lhs_scr[s, pl.ds(b, nsub, stride=bsz), :] = u_ref[b, pl.ds(s, nsub, stride=SSM_SUB), :]
    slabs = [lhs_scr[s] for s in range(SSM_SUB)]
    lhs = [jnp.concatenate([x[:, h * hl:(h + 1) * hl] for x in slabs], axis=-1).astype(_BF16)
           for h in range(2)]
    for h in range(2):
        v = jnp.dot(lhs[h], ws_ref[h], preferred_element_type=_F32)
        v_scr[:, h * hs:(h + 1) * hs] = v[:, :hs]
        v_scr[:, ns + h * hs:ns + (h + 1) * hs] = v[:, hs:]

    ar = jnp.broadcast_to(ar_ref[0], (bsz, ns))
    ai = jnp.broadcast_to(ai_ref[0], (bsz, ns))

    def step(c, carry):
        sr, si = carry
        r0 = pl.multiple_of(c * bsz, bsz)
        vr = v_scr[pl.ds(r0, bsz), 0:ns]
        vi = v_scr[pl.ds(r0, bsz), ns:2 * ns]
        v_scr[pl.ds(r0, bsz), 0:ns] = sr
        v_scr[pl.ds(r0, bsz), ns:2 * ns] = si
        return ar * sr - ai * si + vr, ar * si + ai * sr + vi

    sr, si = lax.fori_loop(0, nsub, step, (st_scr[0], st_scr[1]), unroll=4)
    st_scr[0] = sr
    st_scr[1] = si

    half_f = hf // 2
    yp = []
    for h in range(2):
        s_prev = jnp.concatenate([v_scr[:, h * hs:(h + 1) * hs],
                                  v_scr[:, ns + h * hs:ns + (h + 1) * hs]], axis=-1).astype(_BF16)
        carry_in = jnp.dot(s_prev, wg_ref[h], preferred_element_type=_F32)
        lo = carry_in[:, :half_f] + jnp.dot(lhs[h][:, :half_f], wt_ref[h, :half_f, :half_f],
                                            preferred_element_type=_F32)
        hi = carry_in[:, half_f:] + jnp.dot(lhs[h], wt_ref[h, :, half_f:], preferred_element_type=_F32)
        yp.append(jnp.concatenate([lo, hi], axis=-1))
    for t in range(SSM_SUB):
        yp_scr[t] = jnp.concatenate([yp[h][:, t * hl:(t + 1) * hl] for h in range(2)], axis=-1)
    for b in range(bsz):
        for t in range(SSM_SUB):
            y_ref[b, pl.ds(t, nsub, stride=SSM_SUB), :] = yp_scr[t, pl.ds(b, nsub, stride=bsz), :]


def _ssm(u, w_t, w_s, w_g, a_r, a_i, sblk):
    bsz, seq, _ = u.shape
    rows = (sblk // SSM_SUB) * bsz
    hf = SSM_SUB * SSM_HALF * SSM_GROUP
    tok = pl.BlockSpec((bsz, sblk, LANES), lambda g, i: (0, i, g))
    wspec = pl.BlockSpec((2, hf, hf), lambda g, i: (g, 0, 0))
    aspec = pl.BlockSpec((1, 1, SSM_NS), lambda g, i: (g, 0, 0))
    return pl.pallas_call(
        _ssm_kernel,
        grid=(SSM_NB, seq // sblk),
        in_specs=[tok, wspec, wspec, wspec, aspec, aspec],
        out_specs=tok,
        out_shape=jax.ShapeDtypeStruct(u.shape, _F32),
        scratch_shapes=[pltpu.VMEM((SSM_SUB, rows, LANES), _F32),
                        pltpu.VMEM((rows, 2 * SSM_NS), _F32),
                        pltpu.VMEM((SSM_SUB, rows, LANES), _F32),
                        pltpu.VMEM((2, bsz, SSM_NS), _F32)],
        compiler_params=pltpu.CompilerParams(dimension_semantics=("parallel", "arbitrary"),
                                             vmem_limit_bytes=VMEM_LIMIT),
        name="ssm",
    )(u, w_t, w_s, w_g, a_r, a_i)


def _attn_kernel(qt_ref, k_ref, vt_ref, o_ref, s_ref, acc_ref, pen_ref):
    t = ATT_TILE
    nq = qt_ref.shape[1]
    nh = k_ref.shape[1]
    ones = jnp.ones((ONES_ROWS, t), _BF16)

    row = lax.broadcasted_iota(jnp.int32, (HEAD_PAD, t), 0) - (QK_NOPE + QK_ROPE)
    qry_chunk = lax.broadcasted_iota(jnp.int32, (HEAD_PAD, t), 1) // CHUNK
    pen_ref[0] = jnp.zeros((HEAD_PAD, t), _BF16)
    pen_ref[1] = jnp.where((row >= 0) & (row < t // CHUNK) & (row > qry_chunk), _NEG_BIG, 0.0).astype(_BF16)

    def scores(dst_ref, qi, kj):
        k0 = pl.multiple_of(kj * t, t)
        pen = pen_ref[(kj == qi).astype(jnp.int32)]
        tile_max = []
        for hh in range(nh):
            qt = qt_ref[0, qi, hh * HEAD_PAD:(hh + 1) * HEAD_PAD, :] + pen
            st = jnp.dot(k_ref[0, hh, pl.ds(k0, t), :], qt, preferred_element_type=_F32)
            dst_ref[hh] = st
            tile_max.append(jnp.max(st, axis=0, keepdims=True))
        return tuple(tile_max)

    def step(cur_ref, nxt_ref, carry):
        qi, kj, tmax, ms, accs = carry
        last = kj == qi
        kj_n = jnp.where(last, 0, kj + 1)
        qi_n = jnp.where(last, qi + 1, qi)
        tmax_n = scores(nxt_ref, jnp.minimum(qi_n, nq - 1), kj_n)
        ms_n, accs_n = [], []
        for hh in range(nh):
            m_new = jnp.maximum(ms[hh], tmax[hh])
            alpha = jnp.exp2(ms[hh] - m_new)
            p = jnp.exp2(cur_ref[hh] - m_new).astype(_BF16)
            vte = jnp.concatenate([vt_ref[0, kj, hh * V_HEAD:(hh + 1) * V_HEAD, :], ones], axis=0)
            acc = alpha * accs[hh] + jnp.dot(vte, p, preferred_element_type=_F32)
            acc_ref[qi, hh] = acc
            ms_n.append(jnp.where(last, _NEG_BIG, m_new))
            accs_n.append(jnp.where(last, 0.0, acc))
        return qi_n, kj_n, tmax_n, tuple(ms_n), tuple(accs_n)

    zero = jnp.int32(0)
    bufs = [s_ref.at[i] for i in range(ATT_BUFS)]
    init = (zero, zero, scores(bufs[0], zero, zero),
            tuple(jnp.full((1, t), _NEG_BIG, _F32) for _ in range(nh)),
            tuple(jnp.zeros((V_HEAD + ONES_ROWS, t), _F32) for _ in range(nh)))
    nsteps = nq * (nq + 1) // 2

    def run(count, c):
        for i in range(count):
            c = step(bufs[i % ATT_BUFS], bufs[(i + 1) % ATT_BUFS], c)
        return c

    assert ATT_UNROLL % ATT_BUFS == 0
    carry = lax.fori_loop(0, nsteps // ATT_UNROLL, lambda _, c: run(ATT_UNROLL, c), init)
    run(nsteps % ATT_UNROLL, carry)

    def finish(qi, _):
        outs = [acc_ref[qi, hh, :V_HEAD] / acc_ref[qi, hh, V_HEAD:V_HEAD + 1] for hh in range(nh)]
        o_ref[0, pl.ds(pl.multiple_of(qi * t, t), t), :] = (
            jnp.transpose(jnp.concatenate(outs, axis=0)).astype(_BF16))
        return 0

    lax.fori_loop(0, nq, finish, 0)


def _attn(qt, k, vt):
    bsz, heads, seq, _ = k.shape
    nh = ATT_HEADS
    k_spec = pl.BlockSpec((1, nh, seq, HEAD_PAD), lambda b, h: (b, h, 0, 0))
    tiled = lambda rows: pl.BlockSpec((1, seq // ATT_TILE, nh * rows, ATT_TILE), lambda b, h: (b, 0, h, 0))
    return pl.pallas_call(
        _attn_kernel,
        grid=(bsz, heads // nh),
        in_specs=[tiled(HEAD_PAD), k_spec, tiled(V_HEAD)],
        out_specs=pl.BlockSpec((1, seq, nh * V_HEAD), lambda b, h: (b, 0, h)),
        out_shape=jax.ShapeDtypeStruct((bsz, seq, MLA_WIDTH), _BF16),
        scratch_shapes=[pltpu.VMEM((ATT_BUFS, nh, ATT_TILE, ATT_TILE), _F32),
                        pltpu.VMEM((seq // ATT_TILE, nh, V_HEAD + ONES_ROWS, ATT_TILE), _F32),
                        pltpu.VMEM((2, HEAD_PAD, ATT_TILE), _BF16)],
        compiler_params=pltpu.CompilerParams(dimension_semantics=("parallel", "parallel"),
                                             vmem_limit_bytes=VMEM_LIMIT),
        name="attn",
    )(qt, k, vt)


def _post_kernel(x_ref, gate_ref, y_ref, szs_ref, sgs_ref, o_ref, szm_ref, sgm_ref,
                 wglu_ref, bglu_ref, wbs_ref, wbm_ref, wout_ref, gpost_ref, out_ref):
    ge = _gelu_tanh(y_ref[0]).astype(_BF16)
    gl = jnp.dot(ge, wglu_ref[...], preferred_element_type=_F32) + bglu_ref[...]
    a = (gl[:, :SSM_WIDTH] * _sigmoid(gl[:, SSM_WIDTH:])) * szs_ref[0].astype(_F32)
    ys = jnp.dot(a.astype(_BF16), wbs_ref[...], preferred_element_type=_F32)
    am = o_ref[0].astype(_F32) * szm_ref[0].astype(_F32)
    ym = jnp.dot(am.astype(_BF16), wbm_ref[...], preferred_element_type=_F32)
    merged = sgs_ref[0].astype(_F32) * ys + sgm_ref[0].astype(_F32) * ym
    out = jnp.dot(merged.astype(_BF16), wout_ref[...], preferred_element_type=_F32)
    out_ref[0] = x_ref[0] + gate_ref[0] * (_rms(out) * gpost_ref[...])


def _post(x, gate, y, szs, sgs, o, szm, sgm, wglu, bglu, wbs, wbm, wout, gpost, tm):
    bsz, seq, _ = x.shape
    tok = pl.BlockSpec((1, tm, D_MODEL), lambda b, i: (b, i, 0))
    per_b = pl.BlockSpec((1, 1, D_MODEL), lambda b, i: (b, 0, 0))
    consts = [wglu, bglu, wbs, wbm, wout, gpost]
    return pl.pallas_call(
        _post_kernel,
        grid=(bsz, seq // tm),
        in_specs=[tok, per_b] + [tok] * 6 + [_const_spec(w.shape) for w in consts],
        out_specs=tok,
        out_shape=jax.ShapeDtypeStruct(x.shape, x.dtype),
        compiler_params=pltpu.CompilerParams(dimension_semantics=("parallel", "parallel"),
                                             vmem_limit_bytes=VMEM_LIMIT),
        name="post",
    )(x, gate, y, szs, sgs, o, szm, sgm, *consts)


def _mla_weights(w_q_up, w_kv_up):
    H = MLA_HEADS
    wq = w_q_up.reshape(Q_LORA, H, QK_NOPE + QK_ROPE)
    zq = jnp.zeros((Q_LORA, H, HEAD_PAD - QK_NOPE - QK_ROPE), w_q_up.dtype)
    wq_t = jnp.transpose(jnp.concatenate([wq, zq], axis=-1).reshape(Q_LORA, H * HEAD_PAD))
    wkv = w_kv_up.reshape(KV_LORA, H, QK_NOPE + V_HEAD)
    zk = jnp.zeros((KV_LORA, H, HEAD_PAD - QK_NOPE), w_kv_up.dtype)
    wk_main = jnp.concatenate([wkv[..., :QK_NOPE], zk], axis=-1).reshape(KV_LORA, H * HEAD_PAD)
    wv_t = jnp.transpose(wkv[..., QK_NOPE:].reshape(KV_LORA, H * V_HEAD))
    return [w.astype(_BF16) for w in (wq_t, wk_main, wv_t)]


def _rope_inv_freq_col():
    inv_freq = ROPE_BASE ** (-jnp.arange(0, QK_ROPE, 2, dtype=_F32) / QK_ROPE)
    return inv_freq.reshape(QK_ROPE // 2, 1)


def kernel(x, c, positions, w_ada, b_ada, g_pre, w_in, ssm_log_dt, ssm_a_re, ssm_a_im, ssm_b_re, ssm_b_im, ssm_c_re, ssm_c_im, ssm_d, w_glu, b_glu, g_q_norm, w_q_up, g_kv_norm, w_kv_up, w_br_ssm, w_br_mla, w_out, g_post):
    bsz, seq, _ = x.shape
    depth = w_ada.shape[0]
    assert bsz == SUBLANES and seq % ATT_TILE == 0
    tm = ATT_TILE
    sblk = min(SSM_BLOCK, seq)
    pos3 = positions.reshape(bsz, 1, seq)
    invf = _rope_inv_freq_col()
    off = np.cumsum((0, SSM_WIDTH, SSM_WIDTH, Q_LORA, KV_LORA, QK_ROPE, MLA_WIDTH, D_MODEL, D_MODEL))

    for l in range(depth):
        mod = _ada(c, w_ada[l], b_ada[l])
        shift, scale, gate = (mod[:, None, i * D_MODEL:(i + 1) * D_MODEL] for i in range(3))

        seg = lambda i, j=None: w_in[l][:, off[i]:off[(i if j is None else j) + 1]].astype(_BF16)
        w_lat = seg(2, 3)
        w_krt = jnp.concatenate(
            [jnp.transpose(seg(4)), jnp.zeros((HEAD_PAD - QK_ROPE, D_MODEL), _BF16)], axis=0)
        ws = [seg(0), seg(1), seg(5), seg(6), seg(7), w_lat, w_krt,
              g_q_norm[l].reshape(1, Q_LORA), g_kv_norm[l].reshape(1, KV_LORA)]
        ws += _mla_weights(w_q_up[l], w_kv_up[l])
        u, szs, szm, sgs, sgm, qt, k, vt = _inproj(
            x, shift, scale, g_pre[l].reshape(1, D_MODEL), pos3, invf, ws, tm)

        w_t, w_s, w_g, a_r, a_i = _ssm_weights(
            ssm_log_dt[l], ssm_a_re[l], ssm_a_im[l], ssm_b_re[l], ssm_b_im[l],
            ssm_c_re[l], ssm_c_im[l], ssm_d[l])
        y = _ssm(u, w_t, w_s, w_g, a_r, a_i, sblk)

        o = _attn(qt, k, vt)

        x = _post(x, gate, y, szs, sgs, o, szm, sgm,
                  w_glu[l].astype(_BF16), b_glu[l].reshape(1, -1), w_br_ssm[l].astype(_BF16),
                  w_br_mla[l].astype(_BF16), w_out[l].astype(_BF16), g_post[l].reshape(1, D_MODEL),
                  min(POST_TILE, seq))
    return x
```

```python
import functools
import math

import jax
import jax.numpy as jnp
import numpy as np
from jax import lax
from jax.experimental import pallas as pl
from jax.experimental.pallas import tpu as pltpu

D_MODEL = 1024
CHUNK = 64
SSM_WIDTH = 1024
SSM_GROUP = 16
SSM_GROUPS = SSM_WIDTH // SSM_GROUP
SSM_STATE = 64
MLA_HEADS = 16
QK_NOPE = 64
QK_ROPE = 32
V_HEAD = 64
Q_LORA = 256
KV_LORA = 256
MLA_WIDTH = MLA_HEADS * V_HEAD
ROPE_BASE = 10000.0
EPS = 1e-6

LANES = 128
SUBLANES = 8
HEAD_PAD = 128
SSM_SUB = 8
SSM_BUNDLE = LANES // SSM_GROUP
SSM_NB = SSM_GROUPS // SSM_BUNDLE
SSM_HALF = SSM_BUNDLE // 2
SSM_NS = SSM_BUNDLE * SSM_STATE
SSM_BLOCK = 1024
ATT_TILE = 256
ATT_HEADS = 4
ATT_UNROLL = 8
ATT_BUFS = 2
POST_TILE = 512
ONES_ROWS = 16
VMEM_LIMIT = 56 * 1024 * 1024

_F32 = jnp.float32
_BF16 = jnp.bfloat16
_SOFTMAX_SCALE_LOG2E = ((QK_NOPE + QK_ROPE) ** -0.5) * math.log2(math.e)
_NEG_BIG = -1e30


def _const_spec(shape):
    zeros = (0,) * len(shape)
    return pl.BlockSpec(shape, lambda *_: zeros, pipeline_mode=pl.Buffered(1))


def _sigmoid(x):
    return 1.0 / (1.0 + jnp.exp(-x))


def _gelu_tanh(x):
    return 0.5 * x * (1.0 + jnp.tanh(math.sqrt(2.0 / math.pi) * (x + 0.044715 * (x * x * x))))


def _rms(x):
    return x * lax.rsqrt(jnp.mean(x * x, axis=-1, keepdims=True) + EPS)


def _ada_kernel(c_ref, w_ref, b_ref, o_ref):
    o_ref[...] = jnp.dot(c_ref[...], w_ref[...], preferred_element_type=_F32) + b_ref[...]


def _ada(c, w, b):
    bsz = c.shape[0]
    n = w.shape[1]
    bn = D_MODEL
    return pl.pallas_call(
        _ada_kernel,
        grid=(n // bn,),
        in_specs=[pl.BlockSpec((bsz, D_MODEL), lambda j: (0, 0)),
                  pl.BlockSpec((D_MODEL, bn), lambda j: (0, j)),
                  pl.BlockSpec((1, bn), lambda j: (0, j))],
        out_specs=pl.BlockSpec((bsz, bn), lambda j: (0, j)),
        out_shape=jax.ShapeDtypeStruct((bsz, n), _F32),
        compiler_params=pltpu.CompilerParams(dimension_semantics=("arbitrary",),
                                             vmem_limit_bytes=VMEM_LIMIT),
        name="ada",
    )(c, w, b.reshape(1, n))


def _inproj_kernel(x_ref, shift_ref, scale_ref, gpre_ref, pos_ref, invf_ref,
                   w_u_ref, w_zs_ref, w_zm_ref, w_gs_ref, w_gm_ref, w_lat_ref, w_krt_ref,
                   gq_ref, gkv_ref, wqt_ref, wk_ref, wvt_ref,
                   u_ref, szs_ref, szm_ref, sgs_ref, sgm_ref, qt_ref, k_ref, vt_ref):
    tm = x_ref.shape[1]
    half = QK_ROPE // 2
    r1 = slice(QK_NOPE, QK_NOPE + half)
    r2 = slice(QK_NOPE + half, QK_NOPE + QK_ROPE)
    nt = (((1,), (1,)), ((), ()))
    x = x_ref[0]
    h = _rms(x) * gpre_ref[...] * (1.0 + scale_ref[0]) + shift_ref[0]
    hb = h.astype(_BF16)

    def proj(w_ref):
        return jnp.dot(hb, w_ref[...], preferred_element_type=_F32)

    u_ref[0] = proj(w_u_ref)
    z = proj(w_zs_ref)
    szs_ref[0] = (z * _sigmoid(z)).astype(_BF16)
    z = proj(w_zm_ref)
    szm_ref[0] = (z * _sigmoid(z)).astype(_BF16)
    sgs_ref[0] = _sigmoid(proj(w_gs_ref)).astype(_BF16)
    sgm_ref[0] = _sigmoid(proj(w_gm_ref)).astype(_BF16)

    lat = proj(w_lat_ref)
    qn = (_rms(lat[:, :Q_LORA]) * gq_ref[...]).astype(_BF16)
    kvn = (_rms(lat[:, Q_LORA:]) * gkv_ref[...]).astype(_BF16)

    ang = invf_ref[...] * pos_ref[0].astype(_F32)
    cos = jnp.cos(ang)
    sin = jnp.sin(ang)

    qt = lax.dot_general(wqt_ref[...], qn, nt, preferred_element_type=_F32)
    for hd in range(MLA_HEADS):
        blk = qt[hd * HEAD_PAD:(hd + 1) * HEAD_PAD]
        t1, t2 = blk[r1], blk[r2]
        blk = jnp.concatenate([blk[:QK_NOPE], t1 * cos - t2 * sin, t2 * cos + t1 * sin,
                               blk[QK_NOPE + QK_ROPE:]], axis=0)
        qt_ref[0, 0, hd * HEAD_PAD:(hd + 1) * HEAD_PAD, :] = (blk * _SOFTMAX_SCALE_LOG2E).astype(_BF16)

    krt = lax.dot_general(w_krt_ref[...], hb, nt, preferred_element_type=_F32)
    t1, t2 = krt[:half], krt[half:QK_ROPE]
    spare = HEAD_PAD - QK_NOPE - QK_ROPE
    row = lax.broadcasted_iota(jnp.int32, (spare, tm), 0)
    col_chunk = lax.broadcasted_iota(jnp.int32, (spare, tm), 1) // CHUNK
    onehot = jnp.where(row == col_chunk, 1.0, 0.0)
    kr_full = jnp.concatenate([jnp.zeros((QK_NOPE, tm), _F32), t1 * cos - t2 * sin,
                               t2 * cos + t1 * sin, onehot], axis=0)
    kr_place = jnp.transpose(kr_full)
    ka = jnp.dot(kvn, wk_ref[...], preferred_element_type=_F32)
    for hd in range(MLA_HEADS):
        k_ref[0, hd] = (ka[:, hd * HEAD_PAD:(hd + 1) * HEAD_PAD] + kr_place).astype(_BF16)
    vt_ref[0, 0] = lax.dot_general(wvt_ref[...], kvn, nt, preferred_element_type=_F32).astype(_BF16)


def _inproj(x, shift, scale, g_pre, pos3, invf, ws, tm):
    bsz, seq, _ = x.shape
    tok = lambda w: pl.BlockSpec((1, tm, w), lambda b, i: (b, i, 0))
    per_b = pl.BlockSpec((1, 1, D_MODEL), lambda b, i: (b, 0, 0))
    head = pl.BlockSpec((1, MLA_HEADS, tm, HEAD_PAD), lambda b, i: (b, 0, i, 0))
    tiled = lambda rows: pl.BlockSpec((1, 1, rows, tm), lambda b, i: (b, i, 0, 0))
    tshape = lambda rows: jax.ShapeDtypeStruct((bsz, seq // tm, rows, tm), _BF16)
    act = jax.ShapeDtypeStruct((bsz, seq, D_MODEL), _BF16)
    w_specs = [_const_spec(w.shape) for w in ws]
    return pl.pallas_call(
        _inproj_kernel,
        grid=(bsz, seq // tm),
        in_specs=[tok(D_MODEL), per_b, per_b, _const_spec((1, D_MODEL)),
                  pl.BlockSpec((1, 1, tm), lambda b, i: (b, 0, i)),
                  _const_spec((QK_ROPE // 2, 1))] + w_specs,
        out_specs=[tok(D_MODEL)] * 5 + [tiled(MLA_HEADS * HEAD_PAD), head, tiled(MLA_WIDTH)],
        out_shape=[jax.ShapeDtypeStruct((bsz, seq, SSM_WIDTH), _F32)] + [act] * 4
                  + [tshape(MLA_HEADS * HEAD_PAD),
                     jax.ShapeDtypeStruct((bsz, MLA_HEADS, seq, HEAD_PAD), _BF16), tshape(MLA_WIDTH)],
        compiler_params=pltpu.CompilerParams(dimension_semantics=("parallel", "parallel"),
                                             vmem_limit_bytes=VMEM_LIMIT),
        name="inproj",
    )(x, shift, scale, g_pre, pos3, invf, *ws)


def _ssm_weights(log_dt, a_re, a_im, b_re, b_im, c_re, c_im, d_skip):
    hi = lax.Precision.HIGHEST
    L, G, N, P = SSM_SUB, SSM_GROUPS, SSM_STATE, SSM_GROUP
    nb, gb = G // SSM_HALF, SSM_HALF
    dt = jnp.exp(log_dt.astype(_F32))[:, None]
    lr, li = a_re.astype(_F32), a_im.astype(_F32)
    mag = jnp.exp(lr * dt)
    abar_re, abar_im = mag * jnp.cos(li * dt), mag * jnp.sin(li * dt)
    den = lr * lr + li * li
    nr, ni = abar_re - 1.0, abar_im
    fr = (nr * lr + ni * li) / den
    fi = (ni * lr - nr * li) / den
    br, bi = b_re.astype(_F32), b_im.astype(_F32)
    bbr = fr[..., None] * br - fi[..., None] * bi
    bbi = fr[..., None] * bi + fi[..., None] * br
    cr, ci = c_re.astype(_F32), c_im.astype(_F32)
    j = jnp.arange(L + 1, dtype=_F32)[:, None, None]
    pmag = jnp.exp(lr * dt * j)
    pr, pi_ = pmag * jnp.cos(li * dt * j), pmag * jnp.sin(li * dt * j)
    bbr_t, bbi_t = jnp.swapaxes(bbr, 1, 2), jnp.swapaxes(bbi, 1, 2)
    abr = pr[:L, :, None, :] * bbr_t - pi_[:L, :, None, :] * bbi_t
    abi = pr[:L, :, None, :] * bbi_t + pi_[:L, :, None, :] * bbr_t
    kj = (jnp.einsum('gpn,jgqn->gqjp', cr, abr, precision=hi)
          - jnp.einsum('gpn,jgqn->gqjp', ci, abi, precision=hi))
    kj = kj.at[:, :, 0, :].add(jnp.eye(P, dtype=_F32)[None] * d_skip.astype(_F32).reshape(G, 1, P))
    kj = kj.reshape(G, P, L * P)

    def by_half_slab(x):
        lead, _, rows, width = x.shape
        x = x.reshape(lead, nb, gb, rows, width)
        return jnp.transpose(x, (1, 0, 2, 3, 4)).reshape(nb, lead * gb * rows, width)

    t_c = by_half_slab(jnp.stack(
        [jnp.pad(kj, ((0, 0), (0, 0), (s * P, 0)))[:, :, :L * P] for s in range(L)], axis=0))
    s_c = by_half_slab(jnp.concatenate([abr[::-1], abi[::-1]], axis=-1))
    c_rep = lambda c: jnp.tile(jnp.swapaxes(c, 1, 2), (1, 1, L))
    p_rep = lambda a: jnp.repeat(jnp.transpose(a[1:], (1, 2, 0)), P, axis=-1)
    crr, cir, prr, pir = c_rep(cr), c_rep(ci), p_rep(pr), p_rep(pi_)
    g_c = by_half_slab(jnp.stack([crr * prr - cir * pir, -(crr * pir + cir * prr)], axis=0))

    def spread(compact, col_block, row_block):
        width = compact.shape[-1] * gb
        lane = np.arange(width)
        src = (lane // (gb * col_block)) * col_block + lane % col_block
        onehot = jnp.asarray(np.arange(compact.shape[-1])[:, None] == src[None, :], _BF16)
        full = jnp.einsum('brk,kc->brc', compact.astype(_BF16), onehot, preferred_element_type=_F32)
        row_g = (np.arange(compact.shape[1]) // row_block) % gb
        col_g = (lane // col_block) % gb
        return jnp.where(jnp.asarray(row_g[:, None] == col_g[None, :]), full, 0.0).astype(_BF16)

    w_t = spread(t_c, P, P)
    w_s = spread(s_c, N, P)
    w_g = spread(g_c, P, N)
    a_r = pr[L].reshape(SSM_NB, 1, SSM_NS)
    a_i = pi_[L].reshape(SSM_NB, 1, SSM_NS)
    return w_t, w_s, w_g, a_r, a_i


def _ssm_kernel(u_ref, wt_ref, ws_ref, wg_ref, ar_ref, ai_ref, y_ref, lhs_scr, v_scr, yp_scr, st_scr):
    bsz, sblk, _ = u_ref.shape
    nsub = sblk // SSM_SUB
    ns = SSM_NS
    hl = SSM_HALF * SSM_GROUP
    hs = SSM_HALF * SSM_STATE
    hf = SSM_SUB * hl

    @pl.when(pl.program_id(1) == 0)
    def _():
        st_scr[...] = jnp.zeros_like(st_scr)

    for b in range(bsz):
        for s in range(SSM_SUB):
            lhs_scr[s, pl.ds(b, nsub, stride=bsz), :] = u_ref[b, pl.ds(s, nsub, stride=SSM_SUB), :]
    slabs = [lhs_scr[s] for s in range(SSM_SUB)]
    lhs = [jnp.concatenate([x[:, h * hl:(h + 1) * hl] for x in slabs], axis=-1).astype(_BF16)
           for h in range(2)]
    for h in range(2):
        v = jnp.dot(lhs[h], ws_ref[h], preferred_element_type=_F32)
        v_scr[:, h * hs:(h + 1) * hs] = v[:, :hs]
        v_scr[:, ns + h * hs:ns + (h + 1) * hs] = v[:, hs:]

    ar = jnp.broadcast_to(ar_ref[0], (bsz, ns))
    ai = jnp.broadcast_to(ai_ref[0], (bsz, ns))

    def step(c, carry):
        sr, si = carry
        r0 = pl.multiple_of(c * bsz, bsz)
        vr = v_scr[pl.ds(r0, bsz), 0:ns]
        vi = v_scr[pl.ds(r0, bsz), ns:2 * ns]
        v_scr[pl.ds(r0, bsz), 0:ns] = sr
        v_scr[pl.ds(r0, bsz), ns:2 * ns] = si
        return ar * sr - ai * si + vr, ar * si + ai * sr + vi

    sr, si = lax.fori_loop(0, nsub, step, (st_scr[0], st_scr[1]), unroll=4)
    st_scr[0] = sr
    st_scr[1] = si

    half_f = hf // 2
    yp = []
    for h in range(2):
        s_prev = jnp.concatenate([v_scr[:, h * hs:(h + 1) * hs],
                                  v_scr[:, ns + h * hs:ns + (h + 1) * hs]], axis=-1).astype(_BF16)
        carry_in = jnp.dot(s_prev, wg_ref[h], preferred_element_type=_F32)
        lo = carry_in[:, :half_f] + jnp.dot(lhs[h][:, :half_f], wt_ref[h, :half_f, :half_f],
                                            preferred_element_type=_F32)
        hi = carry_in[:, half_f:] + jnp.dot(lhs[h], wt_ref[h, :, half_f:], preferred_element_type=_F32)
        yp.append(jnp.concatenate([lo, hi], axis=-1))
    for t in range(SSM_SUB):
        yp_scr[t] = jnp.concatenate([yp[h][:, t * hl:(t + 1) * hl] for h in range(2)], axis=-1)
    for b in range(bsz):
        for t in range(SSM_SUB):
            y_ref[b, pl.ds(t, nsub, stride=SSM_SUB), :] = yp_scr[t, pl.ds(b, nsub, stride=bsz), :]


def _ssm(u, w_t, w_s, w_g, a_r, a_i, sblk):
    bsz, seq, _ = u.shape
    rows = (sblk // SSM_SUB) * bsz
    hf = SSM_SUB * SSM_HALF * SSM_GROUP
    tok = pl.BlockSpec((bsz, sblk, LANES), lambda g, i: (0, i, g))
    wspec = pl.BlockSpec((2, hf, hf), lambda g, i: (g, 0, 0))
    aspec = pl.BlockSpec((1, 1, SSM_NS), lambda g, i: (g, 0, 0))
    return pl.pallas_call(
        _ssm_kernel,
        grid=(SSM_NB, seq // sblk),
        in_specs=[tok, wspec, wspec, wspec, aspec, aspec],
        out_specs=tok,
        out_shape=jax.ShapeDtypeStruct(u.shape, _F32),
        scratch_shapes=[pltpu.VMEM((SSM_SUB, rows, LANES), _F32),
                        pltpu.VMEM((rows, 2 * SSM_NS), _F32),
                        pltpu.VMEM((SSM_SUB, rows, LANES), _F32),
                        pltpu.VMEM((2, bsz, SSM_NS), _F32)],
        compiler_params=pltpu.CompilerParams(dimension_semantics=("parallel", "arbitrary"),
                                             vmem_limit_bytes=VMEM_LIMIT),
        name="ssm",
    )(u, w_t, w_s, w_g, a_r, a_i)


def _attn_kernel(qt_ref, k_ref, vt_ref, o_ref, s_ref, acc_ref, pen_ref):
    t = ATT_TILE
    nq = qt_ref.shape[1]
    nh = k_ref.shape[1]
    ones = jnp.ones((ONES_ROWS, t), _BF16)

    row = lax.broadcasted_iota(jnp.int32, (HEAD_PAD, t), 0) - (QK_NOPE + QK_ROPE)
    qry_chunk = lax.broadcasted_iota(jnp.int32, (HEAD_PAD, t), 1) // CHUNK
    pen_ref[0] = jnp.zeros((HEAD_PAD, t), _BF16)
    pen_ref[1] = jnp.where((row >= 0) & (row < t // CHUNK) & (row > qry_chunk), _NEG_BIG, 0.0).astype(_BF16)

    def scores(dst_ref, qi, kj):
        k0 = pl.multiple_of(kj * t, t)
        pen = pen_ref[(kj == qi).astype(jnp.int32)]
        tile_max = []
        for hh in range(nh):
            qt = qt_ref[0, qi, hh * HEAD_PAD:(hh + 1) * HEAD_PAD, :] + pen
            st = jnp.dot(k_ref[0, hh, pl.ds(k0, t), :], qt, preferred_element_type=_F32)
            dst_ref[hh] = st
            tile_max.append(jnp.max(st, axis=0, keepdims=True))
        return tuple(tile_max)

    def step(cur_ref, nxt_ref, carry):
        qi, kj, tmax, ms, accs = carry
        last = kj == qi
        kj_n = jnp.where(last, 0, kj + 1)
        qi_n = jnp.where(last, qi + 1, qi)
        tmax_n = scores(nxt_ref, jnp.minimum(qi_n, nq - 1), kj_n)
        ms_n, accs_n = [], []
        for hh in range(nh):
            m_new = jnp.maximum(ms[hh], tmax[hh])
            alpha = jnp.exp2(ms[hh] - m_new)
            p = jnp.exp2(cur_ref[hh] - m_new).astype(_BF16)
            vte = jnp.concatenate([vt_ref[0, kj, hh * V_HEAD:(hh + 1) * V_HEAD, :], ones], axis=0)
            acc = alpha * accs[hh] + jnp.dot(vte, p, preferred_element_type=_F32)
            acc_ref[qi, hh] = acc
            ms_n.append(jnp.where(last, _NEG_BIG, m_new))
            accs_n.append(jnp.where(last, 0.0, acc))
        return qi_n, kj_n, tmax_n, tuple(ms_n), tuple(accs_n)

    zero = jnp.int32(0)
    bufs = [s_ref.at[i] for i in range(ATT_BUFS)]
    init = (zero, zero, scores(bufs[0], zero, zero),
            tuple(jnp.full((1, t), _NEG_BIG, _F32) for _ in range(nh)),
            tuple(jnp.zeros((V_HEAD + ONES_ROWS, t), _F32) for _ in range(nh)))
    nsteps = nq * (nq + 1) // 2

    def run(count, c):
        for i in range(count):
            c = step(bufs[i % ATT_BUFS], bufs[(i + 1) % ATT_BUFS], c)
        return c

    assert ATT_UNROLL % ATT_BUFS == 0
    carry = lax.fori_loop(0, nsteps // ATT_UNROLL, lambda _, c: run(ATT_UNROLL, c), init)
    run(nsteps % ATT_UNROLL, carry)

    def finish(qi, _):
        outs = [acc_ref[qi, hh, :V_HEAD] / acc_ref[qi, hh, V_HEAD:V_HEAD + 1] for hh in range(nh)]
        o_ref[0, qi] = jnp.concatenate(outs, axis=0).astype(_BF16)
        return 0

    lax.fori_loop(0, nq, finish, 0)


def _attn(qt, k, vt):
    bsz, heads, seq, _ = k.shape
    nh = ATT_HEADS
    k_spec = pl.BlockSpec((1, nh, seq, HEAD_PAD), lambda b, h: (b, h, 0, 0))
    tiled = lambda rows: pl.BlockSpec((1, seq // ATT_TILE, nh * rows, ATT_TILE), lambda b, h: (b, 0, h, 0))
    return pl.pallas_call(
        _attn_kernel,
        grid=(bsz, heads // nh),
        in_specs=[tiled(HEAD_PAD), k_spec, tiled(V_HEAD)],
        out_specs=tiled(V_HEAD),
        out_shape=jax.ShapeDtypeStruct((bsz, seq // ATT_TILE, MLA_WIDTH, ATT_TILE), _BF16),
        scratch_shapes=[pltpu.VMEM((ATT_BUFS, nh, ATT_TILE, ATT_TILE), _F32),
                        pltpu.VMEM((seq // ATT_TILE, nh, V_HEAD + ONES_ROWS, ATT_TILE), _F32),
                        pltpu.VMEM((2, HEAD_PAD, ATT_TILE), _BF16)],
        compiler_params=pltpu.CompilerParams(dimension_semantics=("parallel", "parallel"),
                                             vmem_limit_bytes=VMEM_LIMIT),
        name="attn",
    )(qt, k, vt)


def _post_kernel(x_ref, gate_ref, y_ref, szs_ref, sgs_ref, o_ref, szm_ref, sgm_ref,
                 wglu_ref, bglu_ref, wbs_ref, wbm_ref, wout_ref, gpost_ref, out_ref):
    ge = _gelu_tanh(y_ref[0]).astype(_BF16)
    gl = jnp.dot(ge, wglu_ref[...], preferred_element_type=_F32) + bglu_ref[...]
    a = (gl[:, :SSM_WIDTH] * _sigmoid(gl[:, SSM_WIDTH:])) * szs_ref[0].astype(_F32)
    ys = jnp.dot(a.astype(_BF16), wbs_ref[...], preferred_element_type=_F32)
    o = jnp.concatenate([jnp.transpose(o_ref[0, j].astype(_F32)) for j in range(o_ref.shape[1])], axis=0)
    am = o * szm_ref[0].astype(_F32)
    ym = jnp.dot(am.astype(_BF16), wbm_ref[...], preferred_element_type=_F32)
    merged = sgs_ref[0].astype(_F32) * ys + sgm_ref[0].astype(_F32) * ym
    out = jnp.dot(merged.astype(_BF16), wout_ref[...], preferred_element_type=_F32)
    out_ref[0] = x_ref[0] + gate_ref[0] * (_rms(out) * gpost_ref[...])


def _post(x, gate, y, szs, sgs, o, szm, sgm, wglu, bglu, wbs, wbm, wout, gpost, tm):
    bsz, seq, _ = x.shape
    tok = pl.BlockSpec((1, tm, D_MODEL), lambda b, i: (b, i, 0))
    per_b = pl.BlockSpec((1, 1, D_MODEL), lambda b, i: (b, 0, 0))
    consts = [wglu, bglu, wbs, wbm, wout, gpost]
    return pl.pallas_call(
        _post_kernel,
        grid=(bsz, seq // tm),
        in_specs=[tok, per_b] + [tok] * 3
                 + [pl.BlockSpec((1, tm // ATT_TILE, MLA_WIDTH, ATT_TILE), lambda b, i: (b, i, 0, 0))]
                 + [tok] * 2 + [_const_spec(w.shape) for w in consts],
        out_specs=tok,
        out_shape=jax.ShapeDtypeStruct(x.shape, x.dtype),
        compiler_params=pltpu.CompilerParams(dimension_semantics=("parallel", "parallel"),
                                             vmem_limit_bytes=VMEM_LIMIT),
        name="post",
    )(x, gate, y, szs, sgs, o, szm, sgm, *consts)


def _mla_weights(w_q_up, w_kv_up):
    H = MLA_HEADS
    wq = w_q_up.reshape(Q_LORA, H, QK_NOPE + QK_ROPE)
    zq = jnp.zeros((Q_LORA, H, HEAD_PAD - QK_NOPE - QK_ROPE), w_q_up.dtype)
    wq_t = jnp.transpose(jnp.concatenate([wq, zq], axis=-1).reshape(Q_LORA, H * HEAD_PAD))
    wkv = w_kv_up.reshape(KV_LORA, H, QK_NOPE + V_HEAD)
    zk = jnp.zeros((KV_LORA, H, HEAD_PAD - QK_NOPE), w_kv_up.dtype)
    wk_main = jnp.concatenate([wkv[..., :QK_NOPE], zk], axis=-1).reshape(KV_LORA, H * HEAD_PAD)
    wv_t = jnp.transpose(wkv[..., QK_NOPE:].reshape(KV_LORA, H * V_HEAD))
    return [w.astype(_BF16) for w in (wq_t, wk_main, wv_t)]


def _rope_inv_freq_col():
    inv_freq = ROPE_BASE ** (-jnp.arange(0, QK_ROPE, 2, dtype=_F32) / QK_ROPE)
    return inv_freq.reshape(QK_ROPE // 2, 1)


def kernel(x, c, positions, w_ada, b_ada, g_pre, w_in, ssm_log_dt, ssm_a_re, ssm_a_im, ssm_b_re, ssm_b_im, ssm_c_re, ssm_c_im, ssm_d, w_glu, b_glu, g_q_norm, w_q_up, g_kv_norm, w_kv_up, w_br_ssm, w_br_mla, w_out, g_post):
    bsz, seq, _ = x.shape
    depth = w_ada.shape[0]
    assert bsz == SUBLANES and seq % ATT_TILE == 0
    tm = ATT_TILE
    sblk = min(SSM_BLOCK, seq)
    pos3 = positions.reshape(bsz, 1, seq)
    invf = _rope_inv_freq_col()
    off = np.cumsum((0, SSM_WIDTH, SSM_WIDTH, Q_LORA, KV_LORA, QK_ROPE, MLA_WIDTH, D_MODEL, D_MODEL))

    for l in range(depth):
        mod = _ada(c, w_ada[l], b_ada[l])
        shift, scale, gate = (mod[:, None, i * D_MODEL:(i + 1) * D_MODEL] for i in range(3))

        seg = lambda i, j=None: w_in[l][:, off[i]:off[(i if j is None else j) + 1]].astype(_BF16)
        w_lat = seg(2, 3)
        w_krt = jnp.concatenate(
            [jnp.transpose(seg(4)), jnp.zeros((HEAD_PAD - QK_ROPE, D_MODEL), _BF16)], axis=0)
        ws = [seg(0), seg(1), seg(5), seg(6), seg(7), w_lat, w_krt,
              g_q_norm[l].reshape(1, Q_LORA), g_kv_norm[l].reshape(1, KV_LORA)]
        ws += _mla_weights(w_q_up[l], w_kv_up[l])
        u, szs, szm, sgs, sgm, qt, k, vt = _inproj(
            x, shift, scale, g_pre[l].reshape(1, D_MODEL), pos3, invf, ws, tm)

        w_t, w_s, w_g, a_r, a_i = _ssm_weights(
            ssm_log_dt[l], ssm_a_re[l], ssm_a_im[l], ssm_b_re[l], ssm_b_im[l],
            ssm_c_re[l], ssm_c_im[l], ssm_d[l])
        y = _ssm(u, w_t, w_s, w_g, a_r, a_i, sblk)

        o = _attn(qt, k, vt)

        x = _post(x, gate, y, szs, sgs, o, szm, sgm,
                  w_glu[l].astype(_BF16), b_glu[l].reshape(1, -1), w_br_ssm[l].astype(_BF16),
                  w_br_mla[l].astype(_BF16), w_out[l].astype(_BF16), g_post[l].reshape(1, D_MODEL),
                  min(POST_TILE, seq))
    return x
```

```python
import functools
import math

import jax
import jax.numpy as jnp
import numpy as np
from jax import lax
from jax.experimental import pallas as pl
from jax.experimental.pallas import tpu as pltpu

D_MODEL = 1024
CHUNK = 64
SSM_WIDTH = 1024
SSM_GROUP = 16
SSM_GROUPS = SSM_WIDTH // SSM_GROUP
SSM_STATE = 64
MLA_HEADS = 16
QK_NOPE = 64
QK_ROPE = 32
V_HEAD = 64
Q_LORA = 256
KV_LORA = 256
MLA_WIDTH = MLA_HEADS * V_HEAD
ROPE_BASE = 10000.0
EPS = 1e-6

LANES = 128
SUBLANES = 8
HEAD_PAD = 128
SSM_SUB = 8
SSM_BUNDLE = LANES // SSM_GROUP
SSM_NB = SSM_GROUPS // SSM_BUNDLE
SSM_HALF = SSM_BUNDLE // 2
SSM_NS = SSM_BUNDLE * SSM_STATE
SSM_BLOCK = 1024
ATT_TILE = 256
ATT_HEADS = 4
ATT_UNROLL = 8
ATT_BUFS = 2
POST_TILE = 512
ONES_ROWS = 16
VMEM_LIMIT = 56 * 1024 * 1024

_F32 = jnp.float32
_BF16 = jnp.bfloat16
_SOFTMAX_SCALE_LOG2E = ((QK_NOPE + QK_ROPE) ** -0.5) * math.log2(math.e)
_NEG_BIG = -1e30


def _const_spec(shape):
    zeros = (0,) * len(shape)
    return pl.BlockSpec(shape, lambda *_: zeros, pipeline_mode=pl.Buffered(1))


def _sigmoid(x):
    return 1.0 / (1.0 + jnp.exp(-x))


def _gelu_tanh(x):
    return 0.5 * x * (1.0 + jnp.tanh(math.sqrt(2.0 / math.pi) * (x + 0.044715 * (x * x * x))))


def _rms(x):
    return x * lax.rsqrt(jnp.mean(x * x, axis=-1, keepdims=True) + EPS)


def _ada_kernel(c_ref, w_ref, b_ref, o_ref):
    o_ref[...] = jnp.dot(c_ref[...], w_ref[...], preferred_element_type=_F32) + b_ref[...]


def _ada(c, w, b):
    bsz = c.shape[0]
    n = w.shape[1]
    bn = D_MODEL
    return pl.pallas_call(
        _ada_kernel,
        grid=(n // bn,),
        in_specs=[pl.BlockSpec((bsz, D_MODEL), lambda j: (0, 0)),
                  pl.BlockSpec((D_MODEL, bn), lambda j: (0, j)),
                  pl.BlockSpec((1, bn), lambda j: (0, j))],
        out_specs=pl.BlockSpec((bsz, bn), lambda j: (0, j)),
        out_shape=jax.ShapeDtypeStruct((bsz, n), _F32),
        compiler_params=pltpu.CompilerParams(dimension_semantics=("arbitrary",),
                                             vmem_limit_bytes=VMEM_LIMIT),
        name="ada",
    )(c, w, b.reshape(1, n))


def _inproj_kernel(x_ref, shift_ref, scale_ref, gpre_ref, pos_ref, invf_ref,
                   w_u_ref, w_zs_ref, w_zm_ref, w_gs_ref, w_gm_ref, w_lat_ref, w_krt_ref,
                   gq_ref, gkv_ref, wqt_ref, wk_ref, wvt_ref,
                   u_ref, szs_ref, szm_ref, sgs_ref, sgm_ref, qt_ref, k_ref, vt_ref):
    tm = x_ref.shape[1]
    half = QK_ROPE // 2
    r1 = slice(QK_NOPE, QK_NOPE + half)
    r2 = slice(QK_NOPE + half, QK_NOPE + QK_ROPE)
    nt = (((1,), (1,)), ((), ()))
    x = x_ref[0]
    h = _rms(x) * gpre_ref[...] * (1.0 + scale_ref[0]) + shift_ref[0]
    hb = h.astype(_BF16)

    def proj(w_ref):
        return jnp.dot(hb, w_ref[...], preferred_element_type=_F32)

    u_ref[0] = proj(w_u_ref)
    z = proj(w_zs_ref)
    szs_ref[0] = (z * _sigmoid(z)).astype(_BF16)
    z = proj(w_zm_ref)
    szm_ref[0] = (z * _sigmoid(z)).astype(_BF16)
    sgs_ref[0] = _sigmoid(proj(w_gs_ref)).astype(_BF16)
    sgm_ref[0] = _sigmoid(proj(w_gm_ref)).astype(_BF16)

    lat = proj(w_lat_ref)
    qn = (_rms(lat[:, :Q_LORA]) * gq_ref[...]).astype(_BF16)
    kvn = (_rms(lat[:, Q_LORA:]) * gkv_ref[...]).astype(_BF16)

    ang = invf_ref[...] * pos_ref[0].astype(_F32)
    cos = jnp.cos(ang)
    sin = jnp.sin(ang)

    qt = lax.dot_general(wqt_ref[...], qn, nt, preferred_element_type=_F32)
    for hd in range(MLA_HEADS):
        blk = qt[hd * HEAD_PAD:(hd + 1) * HEAD_PAD]
        t1, t2 = blk[r1], blk[r2]
        blk = jnp.concatenate([blk[:QK_NOPE], t1 * cos - t2 * sin, t2 * cos + t1 * sin,
                               blk[QK_NOPE + QK_ROPE:]], axis=0)
        qt_ref[0, 0, hd * HEAD_PAD:(hd + 1) * HEAD_PAD, :] = (blk * _SOFTMAX_SCALE_LOG2E).astype(_BF16)

    krt = lax.dot_general(w_krt_ref[...], hb, nt, preferred_element_type=_F32)
    t1, t2 = krt[:half], krt[half:QK_ROPE]
    spare = HEAD_PAD - QK_NOPE - QK_ROPE
    row = lax.broadcasted_iota(jnp.int32, (spare, tm), 0)
    col_chunk = lax.broadcasted_iota(jnp.int32, (spare, tm), 1) // CHUNK
    onehot = jnp.where(row == col_chunk, 1.0, 0.0)
    kr_full = jnp.concatenate([jnp.zeros((QK_NOPE, tm), _F32), t1 * cos - t2 * sin,
                               t2 * cos + t1 * sin, onehot], axis=0)
    kr_place = jnp.transpose(kr_full)
    ka = jnp.dot(kvn, wk_ref[...], preferred_element_type=_F32)
    for hd in range(MLA_HEADS):
        k_ref[0, hd] = (ka[:, hd * HEAD_PAD:(hd + 1) * HEAD_PAD] + kr_place).astype(_BF16)
    vt_ref[0, 0] = lax.dot_general(wvt_ref[...], kvn, nt, preferred_element_type=_F32).astype(_BF16)


def _inproj(x, shift, scale, g_pre, pos3, invf, ws, tm):
    bsz, seq, _ = x.shape
    tok = lambda w: pl.BlockSpec((1, tm, w), lambda b, i: (b, i, 0))
    per_b = pl.BlockSpec((1, 1, D_MODEL), lambda b, i: (b, 0, 0))
    head = pl.BlockSpec((1, MLA_HEADS, tm, HEAD_PAD), lambda b, i: (b, 0, i, 0))
    tiled = lambda rows: pl.BlockSpec((1, 1, rows, tm), lambda b, i: (b, i, 0, 0))
    tshape = lambda rows: jax.ShapeDtypeStruct((bsz, seq // tm, rows, tm), _BF16)
    act = jax.ShapeDtypeStruct((bsz, seq, D_MODEL), _BF16)
    w_specs = [_const_spec(w.shape) for w in ws]
    return pl.pallas_call(
        _inproj_kernel,
        grid=(bsz, seq // tm),
        in_specs=[tok(D_MODEL), per_b, per_b, _const_spec((1, D_MODEL)),
                  pl.BlockSpec((1, 1, tm), lambda b, i: (b, 0, i)),
                  _const_spec((QK_ROPE // 2, 1))] + w_specs,
        out_specs=[tok(D_MODEL)] * 5 + [tiled(MLA_HEADS * HEAD_PAD), head, tiled(MLA_WIDTH)],
        out_shape=[jax.ShapeDtypeStruct((bsz, seq, SSM_WIDTH), _F32)] + [act] * 4
                  + [tshape(MLA_HEADS * HEAD_PAD),
                     jax.ShapeDtypeStruct((bsz, MLA_HEADS, seq, HEAD_PAD), _BF16), tshape(MLA_WIDTH)],
        compiler_params=pltpu.CompilerParams(dimension_semantics=("parallel", "parallel"),
                                             vmem_limit_bytes=VMEM_LIMIT),
        name="inproj",
    )(x, shift, scale, g_pre, pos3, invf, *ws)


def _ssm_weights(log_dt, a_re, a_im, b_re, b_im, c_re, c_im, d_skip):
    hi = lax.Precision.HIGHEST
    L, G, N, P = SSM_SUB, SSM_GROUPS, SSM_STATE, SSM_GROUP
    nb, gb = G // SSM_HALF, SSM_HALF
    dt = jnp.exp(log_dt.astype(_F32))[:, None]
    lr, li = a_re.astype(_F32), a_im.astype(_F32)
    mag = jnp.exp(lr * dt)
    abar_re, abar_im = mag * jnp.cos(li * dt), mag * jnp.sin(li * dt)
    den = lr * lr + li * li
    nr, ni = abar_re - 1.0, abar_im
    fr = (nr * lr + ni * li) / den
    fi = (ni * lr - nr * li) / den
    br, bi = b_re.astype(_F32), b_im.astype(_F32)
    bbr = fr[..., None] * br - fi[..., None] * bi
    bbi = fr[..., None] * bi + fi[..., None] * br
    cr, ci = c_re.astype(_F32), c_im.astype(_F32)
    j = jnp.arange(L + 1, dtype=_F32)[:, None, None]
    pmag = jnp.exp(lr * dt * j)
    pr, pi_ = pmag * jnp.cos(li * dt * j), pmag * jnp.sin(li * dt * j)
    bbr_t, bbi_t = jnp.swapaxes(bbr, 1, 2), jnp.swapaxes(bbi, 1, 2)
    abr = pr[:L, :, None, :] * bbr_t - pi_[:L, :, None, :] * bbi_t
    abi = pr[:L, :, None, :] * bbi_t + pi_[:L, :, None, :] * bbr_t
    kj = (jnp.einsum('gpn,jgqn->gqjp', cr, abr, precision=hi)
          - jnp.einsum('gpn,jgqn->gqjp', ci, abi, precision=hi))
    kj = kj.at[:, :, 0, :].add(jnp.eye(P, dtype=_F32)[None] * d_skip.astype(_F32).reshape(G, 1, P))
    kj = kj.reshape(G, P, L * P)

    def by_half_slab(x):
        lead, _, rows, width = x.shape
        x = x.reshape(lead, nb, gb, rows, width)
        return jnp.transpose(x, (1, 0, 2, 3, 4)).reshape(nb, lead * gb * rows, width)

    t_c = by_half_slab(jnp.stack(
        [jnp.pad(kj, ((0, 0), (0, 0), (s * P, 0)))[:, :, :L * P] for s in range(L)], axis=0))
    s_c = by_half_slab(jnp.concatenate([abr[::-1], abi[::-1]], axis=-1))
    c_rep = lambda c: jnp.tile(jnp.swapaxes(c, 1, 2), (1, 1, L))
    p_rep = lambda a: jnp.repeat(jnp.transpose(a[1:], (1, 2, 0)), P, axis=-1)
    crr, cir, prr, pir = c_rep(cr), c_rep(ci), p_rep(pr), p_rep(pi_)
    g_c = by_half_slab(jnp.stack([crr * prr - cir * pir, -(crr * pir + cir * prr)], axis=0))

    def spread(compact, col_block, row_block):
        width = compact.shape[-1] * gb
        lane = np.arange(width)
        src = (lane // (gb * col_block)) * col_block + lane % col_block
        onehot = jnp.asarray(np.arange(compact.shape[-1])[:, None] == src[None, :], _BF16)
        full = jnp.einsum('brk,kc->brc', compact.astype(_BF16), onehot, preferred_element_type=_F32)
        row_g = (np.arange(compact.shape[1]) // row_block) % gb
        col_g = (lane // col_block) % gb
        return jnp.where(jnp.asarray(row_g[:, None] == col_g[None, :]), full, 0.0).astype(_BF16)

    w_t = spread(t_c, P, P)
    w_s = spread(s_c, N, P)
    w_g = spread(g_c, P, N)
    a_r = pr[L].reshape(SSM_NB, 1, SSM_NS)
    a_i = pi_[L].reshape(SSM_NB, 1, SSM_NS)
    return w_t, w_s, w_g, a_r, a_i


def _ssm_kernel(u_ref, wt_ref, ws_ref, wg_ref, ar_ref, ai_ref, y_ref, lhs_scr, v_scr, yp_scr, st_scr):
    bsz, sblk, _ = u_ref.shape
    nsub = sblk // SSM_SUB
    ns = SSM_NS
    hl = SSM_HALF * SSM_GROUP
    hs = SSM_HALF * SSM_STATE
    hf = SSM_SUB * hl

    @pl.when(pl.program_id(1) == 0)
    def _():
        st_scr[...] = jnp.zeros_like(st_scr)

    for b in range(bsz):
        for s in range(SSM_SUB):
            lhs_scr[s, pl.ds(b, nsub, stride=bsz), :] = u_ref[b, pl.ds(s, nsub, stride=SSM_SUB), :]
    slabs = [lhs_scr[s] for s in range(SSM_SUB)]
    lhs = [jnp.concatenate([x[:, h * hl:(h + 1) * hl] for x in slabs], axis=-1).astype(_BF16)
           for h in range(2)]
    for h in range(2):
        v = jnp.dot(lhs[h], ws_ref[h], preferred_element_type=_F32)
        v_scr[:, h * hs:(h + 1) * hs] = v[:, :hs]
        v_scr[:, ns + h * hs:ns + (h + 1) * hs] = v[:, hs:]

    ar = jnp.broadcast_to(ar_ref[0], (bsz, ns))
    ai = jnp.broadcast_to(ai_ref[0], (bsz, ns))

    def step(c, carry):
        sr, si = carry
        r0 = pl.multiple_of(c * bsz, bsz)
        vr = v_scr[pl.ds(r0, bsz), 0:ns]
        vi = v_scr[pl.ds(r0, bsz), ns:2 * ns]
        v_scr[pl.ds(r0, bsz), 0:ns] = sr
        v_scr[pl.ds(r0, bsz), ns:2 * ns] = si
        return ar * sr - ai * si + vr, ar * si + ai * sr + vi

    sr, si = lax.fori_loop(0, nsub, step, (st_scr[0], st_scr[1]), unroll=4)
    st_scr[0] = sr
    st_scr[1] = si

    half_f = hf // 2
    yp = []
    for h in range(2):
        s_prev = jnp.concatenate([v_scr[:, h * hs:(h + 1) * hs],
                                  v_scr[:, ns + h * hs:ns + (h + 1) * hs]], axis=-1).astype(_BF16)
        carry_in = jnp.dot(s_prev, wg_ref[h], preferred_element_type=_F32)
        lo = carry_in[:, :half_f] + jnp.dot(lhs[h][:, :half_f], wt_ref[h, :half_f, :half_f],
                                            preferred_element_type=_F32)
        hi = carry_in[:, half_f:] + jnp.dot(lhs[h], wt_ref[h, :, half_f:], preferred_element_type=_F32)
        yp.append(jnp.concatenate([lo, hi], axis=-1))
    for t in range(SSM_SUB):
        yp_scr[t] = jnp.concatenate([yp[h][:, t * hl:(t + 1) * hl] for h in range(2)], axis=-1)
    for b in range(bsz):
        for t in range(SSM_SUB):
            y_ref[b, pl.ds(t, nsub, stride=SSM_SUB), :] = yp_scr[t, pl.ds(b, nsub, stride=bsz), :]


def _ssm(u, w_t, w_s, w_g, a_r, a_i, sblk):
    bsz, seq, _ = u.shape
    rows = (sblk // SSM_SUB) * bsz
    hf = SSM_SUB * SSM_HALF * SSM_GROUP
    tok = pl.BlockSpec((bsz, sblk, LANES), lambda g, i: (0, i, g))
    wspec = pl.BlockSpec((2, hf, hf), lambda g, i: (g, 0, 0))
    aspec = pl.BlockSpec((1, 1, SSM_NS), lambda g, i: (g, 0, 0))
    return pl.pallas_call(
        _ssm_kernel,
        grid=(SSM_NB, seq // sblk),
        in_specs=[tok, wspec, wspec, wspec, aspec, aspec],
        out_specs=tok,
        out_shape=jax.ShapeDtypeStruct(u.shape, _F32),
        scratch_shapes=[pltpu.VMEM((SSM_SUB, rows, LANES), _F32),
                        pltpu.VMEM((rows, 2 * SSM_NS), _F32),
                        pltpu.VMEM((SSM_SUB, rows, LANES), _F32),
                        pltpu.VMEM((2, bsz, SSM_NS), _F32)],
        compiler_params=pltpu.CompilerParams(dimension_semantics=("parallel", "arbitrary"),
                                             vmem_limit_bytes=VMEM_LIMIT),
        name="ssm",
    )(u, w_t, w_s, w_g, a_r, a_i)


def _attn_kernel(qt_ref, k_ref, vt_ref, o_ref, s_ref, acc_ref, pen_ref):
    t = ATT_TILE
    nq = qt_ref.shape[1]
    nh = k_ref.shape[1]
    ones = jnp.ones((ONES_ROWS, t), _BF16)

    row = lax.broadcasted_iota(jnp.int32, (HEAD_PAD, t), 0) - (QK_NOPE + QK_ROPE)
    qry_chunk = lax.broadcasted_iota(jnp.int32, (HEAD_PAD, t), 1) // CHUNK
    pen_ref[0] = jnp.zeros((HEAD_PAD, t), _BF16)
    pen_ref[1] = jnp.where((row >= 0) & (row < t // CHUNK) & (row > qry_chunk), _NEG_BIG, 0.0).astype(_BF16)

    def scores(dst_ref, qi, kj):
        k0 = pl.multiple_of(kj * t, t)
        pen = pen_ref[(kj == qi).astype(jnp.int32)]
        tile_max = []
        for hh in range(nh):
            qt = qt_ref[0, qi, hh * HEAD_PAD:(hh + 1) * HEAD_PAD, :] + pen
            st = jnp.dot(k_ref[0, hh, pl.ds(k0, t), :], qt, preferred_element_type=_F32)
            dst_ref[hh] = st
            tile_max.append(jnp.max(st, axis=0, keepdims=True))
        return tuple(tile_max)

    def step(cur_ref, nxt_ref, carry):
        qi, kj, tmax, ms, accs = carry
        last = kj == qi
        kj_n = jnp.where(last, 0, kj + 1)
        qi_n = jnp.where(last, qi + 1, qi)
        tmax_n = scores(nxt_ref, jnp.minimum(qi_n, nq - 1), kj_n)
        ms_n, accs_n = [], []
        for hh in range(nh):
            m_new = jnp.maximum(ms[hh], tmax[hh])
            alpha = jnp.exp2(ms[hh] - m_new)
            p = jnp.exp2(cur_ref[hh] - m_new).astype(_BF16)
            vte = jnp.concatenate([vt_ref[0, kj, hh * V_HEAD:(hh + 1) * V_HEAD, :], ones], axis=0)
            acc = alpha * accs[hh] + jnp.dot(vte, p, preferred_element_type=_F32)
            acc_ref[qi, hh] = acc
            ms_n.append(jnp.where(last, _NEG_BIG, m_new))
            accs_n.append(jnp.where(last, 0.0, acc))
        return qi_n, kj_n, tmax_n, tuple(ms_n), tuple(accs_n)

    zero = jnp.int32(0)
    bufs = [s_ref.at[i] for i in range(ATT_BUFS)]
    init = (zero, zero, scores(bufs[0], zero, zero),
            tuple(jnp.full((1, t), _NEG_BIG, _F32) for _ in range(nh)),
            tuple(jnp.zeros((V_HEAD + ONES_ROWS, t), _F32) for _ in range(nh)))
    nsteps = nq * (nq + 1) // 2

    def run(count, c):
        for i in range(count):
            c = step(bufs[i % ATT_BUFS], bufs[(i + 1) % ATT_BUFS], c)
        return c

    assert ATT_UNROLL % ATT_BUFS == 0
    carry = lax.fori_loop(0, nsteps // ATT_UNROLL, lambda _, c: run(ATT_UNROLL, c), init)
    run(nsteps % ATT_UNROLL, carry)

    def finish(qi, _):
        outs = [acc_ref[qi, hh, :V_HEAD] / acc_ref[qi, hh, V_HEAD:V_HEAD + 1] for hh in range(nh)]
        o_ref[0, qi] = jnp.concatenate(outs, axis=0).astype(_BF16)
        return 0

    lax.fori_loop(0, nq, finish, 0)


def _attn(qt, k, vt):
    bsz, heads, seq, _ = k.shape
    nh = ATT_HEADS
    k_spec = pl.BlockSpec((1, nh, seq, HEAD_PAD), lambda b, h: (b, h, 0, 0))
    tiled = lambda rows: pl.BlockSpec((1, seq // ATT_TILE, nh * rows, ATT_TILE), lambda b, h: (b, 0, h, 0))
    return pl.pallas_call(
        _attn_kernel,
        grid=(bsz, heads // nh),
        in_specs=[tiled(HEAD_PAD), k_spec, tiled(V_HEAD)],
        out_specs=tiled(V_HEAD),
        out_shape=jax.ShapeDtypeStruct((bsz, seq // ATT_TILE, MLA_WIDTH, ATT_TILE), _BF16),
        scratch_shapes=[pltpu.VMEM((ATT_BUFS, nh, ATT_TILE, ATT_TILE), _F32),
                        pltpu.VMEM((seq // ATT_TILE, nh, V_HEAD + ONES_ROWS, ATT_TILE), _F32),
                        pltpu.VMEM((2, HEAD_PAD, ATT_TILE), _BF16)],
        compiler_params=pltpu.CompilerParams(dimension_semantics=("parallel", "parallel"),
                                             vmem_limit_bytes=VMEM_LIMIT),
        name="attn",
    )(qt, k, vt)


def _post_kernel(x_ref, gate_ref, y_ref, szs_ref, sgs_ref, o_ref, szm_ref, sgm_ref,
                 wglu_ref, bglu_ref, wbs_ref, wbm_ref, wout_ref, gpost_ref, out_ref):
    ge = _gelu_tanh(y_ref[0]).astype(_BF16)
    gl = jnp.dot(ge, wglu_ref[...], preferred_element_type=_F32) + bglu_ref[...]
    a = (gl[:, :SSM_WIDTH] * _sigmoid(gl[:, SSM_WIDTH:])) * szs_ref[0].astype(_F32)
    ys = jnp.dot(a.astype(_BF16), wbs_ref[...], preferred_element_type=_F32)
    o =jnp.concatenate([jnp.transpose(o_ref[0, j].astype(_F32)) for j in range(o_ref.shape[1])], axis=0)
    am = o * szm_ref[0].astype(_F32)
    ym = jnp.dot(am.astype(_BF16), wbm_ref[...], preferred_element_type=_F32)
    merged = sgs_ref[0].astype(_F32) * ys + sgm_ref[0].astype(_F32) * ym
    out = jnp.dot(merged.astype(_BF16), wout_ref[...], preferred_element_type=_F32)
    out_ref[0] = x_ref[0] + gate_ref[0] * (_rms(out) * gpost_ref[...])


def _post(x, gate, y, szs, sgs, o, szm, sgm, wglu, bglu, wbs, wbm, wout, gpost, tm):
    bsz, seq, _ = x.shape
    tok = pl.BlockSpec((1, tm, D_MODEL), lambda b, i: (b, i, 0))
    per_b = pl.BlockSpec((1, 1, D_MODEL), lambda b, i: (b, 0, 0))
    consts = [wglu, bglu, wbs, wbm, wout, gpost]
    return pl.pallas_call(
        _post_kernel,
        grid=(bsz, seq // tm),
        in_specs=[tok, per_b] + [tok] * 3
                 + [pl.BlockSpec((1, tm // ATT_TILE, MLA_WIDTH, ATT_TILE), lambda b, i: (b, i, 0, 0))]
                 + [tok] * 2 + [_const_spec(w.shape) for w in consts],
        out_specs=tok,
        out_shape=jax.ShapeDtypeStruct(x.shape, x.dtype),
        compiler_params=pltpu.CompilerParams(dimension_semantics=("parallel", "parallel"),
                                             vmem_limit_bytes=VMEM_LIMIT),
        name="post",
    )(x, gate, y, szs, sgs, o, szm, sgm, *consts)


def _mla_weights(w_q_up, w_kv_up):
    H = MLA_HEADS
    wq = w_q_up.reshape(Q_LORA, H, QK_NOPE + QK_ROPE)
    zq = jnp.zeros((Q_LORA, H, HEAD_PAD - QK_NOPE - QK_ROPE), w_q_up.dtype)
    wq_t = jnp.transpose(jnp.concatenate([wq, zq], axis=-1).reshape(Q_LORA, H * HEAD_PAD))
    wkv = w_kv_up.reshape(KV_LORA, H, QK_NOPE + V_HEAD)
    zk = jnp.zeros((KV_LORA, H, HEAD_PAD - QK_NOPE), w_kv_up.dtype)
    wk_main = jnp.concatenate([wkv[..., :QK_NOPE], zk], axis=-1).reshape(KV_LORA, H * HEAD_PAD)
    wv_t = jnp.transpose(wkv[..., QK_NOPE:].reshape(KV_LORA, H * V_HEAD))
    return [w.astype(_BF16) for w in (wq_t, wk_main, wv_t)]


def _rope_inv_freq_col():
    inv_freq = ROPE_BASE ** (-jnp.arange(0, QK_ROPE, 2, dtype=_F32) / QK_ROPE)
    return inv_freq.reshape(QK_ROPE // 2, 1)


def kernel(x, c, positions, w_ada, b_ada, g_pre, w_in, ssm_log_dt, ssm_a_re, ssm_a_im, ssm_b_re, ssm_b_im, ssm_c_re, ssm_c_im, ssm_d, w_glu, b_glu, g_q_norm, w_q_up, g_kv_norm, w_kv_up, w_br_ssm, w_br_mla, w_out, g_post):
    bsz, seq, _ = x.shape
    depth = w_ada.shape[0]
    assert bsz == SUBLANES and seq % ATT_TILE == 0
    tm = ATT_TILE
    sblk = min(SSM_BLOCK, seq)
    pos3 = positions.reshape(bsz, 1, seq)
    invf = _rope_inv_freq_col()
    off = np.cumsum((0, SSM_WIDTH, SSM_WIDTH, Q_LORA, KV_LORA, QK_ROPE, MLA_WIDTH, D_MODEL, D_MODEL))

    for l in range(depth):
        mod = _ada(c, w_ada[l], b_ada[l])
        shift, scale, gate = (mod[:, None, i * D_MODEL:(i + 1) * D_MODEL] for i in range(3))

        seg = lambda i, j=None: w_in[l][:, off[i]:off[(i if j is None else j) + 1]].astype(_BF16)
        w_lat = seg(2, 3)
        w_krt = jnp.concatenate(
            [jnp.transpose(seg(4)), jnp.zeros((HEAD_PAD - QK_ROPE, D_MODEL), _BF16)], axis=0)
        ws = [seg(0), seg(1), seg(5), seg(6), seg(7), w_lat, w_krt,
              g_q_norm[l].reshape(1, Q_LORA), g_kv_norm[l].reshape(1, KV_LORA)]
        ws += _mla_weights(w_q_up[l], w_kv_up[l])
        u, szs, szm, sgs, sgm, qt, k, vt = _inproj(
            x, shift, scale, g_pre[l].reshape(1, D_MODEL), pos3, invf, ws, tm)

        w_t, w_s, w_g, a_r, a_i = _ssm_weights(
            ssm_log_dt[l], ssm_a_re[l], ssm_a_im[l], ssm_b_re[l], ssm_b_im[l],
            ssm_c_re[l], ssm_c_im[l], ssm_d[l])
        y = _ssm(u, w_t, w_s, w_g, a_r, a_i, sblk)

        o = _attn(qt, k, vt)

        x = _post(x, gate, y, szs, sgs, o, szm, sgm,
                  w_glu[l].astype(_BF16), b_glu[l].reshape(1, -1), w_br_ssm[l].astype(_BF16),
                  w_br_mla[l].astype(_BF16), w_out[l].astype(_BF16), g_post[l].reshape(1, D_MODEL),
                  min(POST_TILE, seq))
    return x
```
